```python
import math
import functools
import jax
import jax.numpy as jnp
from jax import lax
import numpy as np

D_MODEL = 1024
BATCH = 32
SEQ = 2048
DEPTH = 1
DEC_BATCH = 128
DEC_SEQ = 4
PAST_LEN = 8192
PAGE_SIZE = 128

HA = 4
DA = 64
DVA = 2 * DA
HB = 8
KVH = 2
GB = HB // KVH
DB = 64
L_CMP = 32
CMP_STRIDE = L_CMP // 2
CMP_HID = 128
SEL_BLOCK = 64
N_SEL = 8
WINDOW = 512
ATTN_Q_BLOCK = 128
SEL_Q_CHUNK = 32
N_GROUPS = 4
EXPERTS_PER_GROUP = 4
N_EXPERTS = N_GROUPS * EXPERTS_PER_GROUP
TOP_K = 2
D_EXPERT = 256
RMS_EPS = 1e-6
NEG_INF = -1e30
BIG = 1e9
POOL_NUM = 5
POOL_DEN = 4
N_IN = 2 * HA * 2 * DA + HA * DVA + HB * DB + 6 * KVH * DB + 3 * HB + 2 * D_MODEL

kernel_name = 'hybrid_diffattn_nsa_hmoe_step'

F32 = jnp.float32


def rms(x, g):
    xf = x.astype(F32)
    y = xf * lax.rsqrt(jnp.mean(xf * xf, axis=-1, keepdims=True) + RMS_EPS)
    return (y * g.astype(F32)).astype(x.dtype)


def alibi_slopes(n):
    return jnp.asarray([2.0 ** (-8.0 * (i + 1) / n) for i in range(n)], dtype=F32)


def masked_softmax(s, mask):
    s = jnp.where(mask, s, NEG_INF)
    m = jnp.max(s, axis=-1, keepdims=True)
    e = jnp.exp(s - m) * mask
    return e / jnp.maximum(jnp.sum(e, axis=-1, keepdims=True), 1e-30)


def split_proj(p):
    sizes = [HA * 2 * DA, HA * 2 * DA, HA * DVA, HB * DB] + [KVH * DB] * 6 + [3 * HB, 2 * D_MODEL]
    cuts = [int(c) for c in np.cumsum(sizes)[:-1]]
    return jnp.split(p, cuts, axis=-1)


def diff_attention(q, k, v, q_pos, k_pos, slopes, lam):
    s = jnp.einsum('bqhcd,bkhcd->bhcqk', q.astype(F32), k.astype(F32)) * (DA ** -0.5)
    dist = (q_pos[:, None] - k_pos[None, :]).astype(F32)
    s = s - slopes[None, :, None, None, None] * dist
    p = masked_softmax(s, k_pos[None, :] <= q_pos[:, None])
    a = p[:, :, 0] - lam * p[:, :, 1]
    return jnp.einsum('bhqk,bkhv->bqhv', a, v.astype(F32)).astype(v.dtype)


def diff_prompt(q, k, v, slopes, lam):
    B, T = q.shape[:2]
    nb = T // ATTN_Q_BLOCK
    qb = q.reshape((B, nb, ATTN_Q_BLOCK) + q.shape[2:]).swapaxes(0, 1)
    k_pos = jnp.arange(T)

    def block(args):
        qi, i = args
        q_pos = i * ATTN_Q_BLOCK + jnp.arange(ATTN_Q_BLOCK)
        return diff_attention(qi, k, v, q_pos, k_pos, slopes, lam)

    o = lax.map(block, (qb, jnp.arange(nb)))
    return o.swapaxes(0, 1).reshape(B, T, HA, DVA)


def gqa_attention(q, k, v, q_pos, k_pos, slopes, window):
    s = jnp.einsum('bqhgd,bkhd->bhgqk', q.astype(F32), k.astype(F32)) * (DB ** -0.5)
    dist = q_pos[:, None] - k_pos[None, :]
    s = s - slopes.reshape(KVH, GB)[None, :, :, None, None] * dist.astype(F32)
    mask = (dist >= 0) & (dist < window) & (k_pos[None, :] >= 0)
    p = masked_softmax(s, mask)
    return jnp.einsum('bhgqk,bkhd->bqhgd', p, v.astype(F32)).astype(v.dtype)


def window_prompt(q, k, v, slopes):
    B, T = q.shape[:2]
    qbs = ATTN_Q_BLOCK
    nb = T // qbs
    nprev = -(-WINDOW // qbs)
    pad = ((0, 0), (nprev * qbs, 0), (0, 0), (0, 0))
    kb = jnp.pad(k, pad).reshape(B, nb + nprev, qbs, KVH, DB)
    vb = jnp.pad(v, pad).reshape(B, nb + nprev, qbs, KVH, DB)
    kband = jnp.concatenate([kb[:, j:j + nb] for j in range(nprev + 1)], axis=2).swapaxes(0, 1)
    vband = jnp.concatenate([vb[:, j:j + nb] for j in range(nprev + 1)], axis=2).swapaxes(0, 1)
    qb = q.reshape(B, nb, qbs, KVH, GB, DB).swapaxes(0, 1)
    q_pos = jnp.arange(T).reshape(nb, qbs)
    k_pos = jnp.arange(nb)[:, None] * qbs - nprev * qbs + jnp.arange((nprev + 1) * qbs)[None, :]

    def block(args):
        qi, ki, vi, qp, kp = args
        return gqa_attention(qi, ki, vi, qp, kp, slopes, WINDOW)

    o = lax.map(block, (qb, kband, vband, q_pos, k_pos))
    return o.swapaxes(0, 1).reshape(B, T, KVH, GB, DB)


def compress(rows, pe, w1, b1, w2, b2, tk_pad):
    B, Tk = rows.shape[:2]
    r = jnp.pad(rows, ((0, 0), (0, tk_pad + CMP_STRIDE - Tk), (0, 0), (0, 0)))
    ch = r.reshape(B, tk_pad // CMP_STRIDE + 1, CMP_STRIDE, KVH, DB)
    ha = jnp.einsum('bnjhd,jde->bnhe', ch + pe[:CMP_STRIDE, None, :], w1[:CMP_STRIDE])
    hb = jnp.einsum('bnjhd,jde->bnhe', ch + pe[CMP_STRIDE:, None, :], w1[CMP_STRIDE:])
    hid = jax.nn.gelu(ha[:, :-1] + hb[:, 1:] + b1)
    return jnp.einsum('bnhe,ed->bnhd', hid, w2) + b2


def nsa_cmp_sel(q, q_pos, kc, vc, ks, vs, W, slopes):
    B, Tq = q.shape[:2]
    Tk = kc.shape[1]
    tk_pad = -(-Tk // SEL_BLOCK) * SEL_BLOCK
    sl = slopes.reshape(KVH, GB)
    kcmp = rms(compress(kc, W['cmp_pe_k'], W['cmp_w1_k'], W['cmp_b1_k'], W['cmp_w2_k'], W['cmp_b2_k'], tk_pad), W['kn_cmp'])
    vcmp = compress(vc, W['cmp_pe_v'], W['cmp_w1_v'], W['cmp_b1_v'], W['cmp_w2_v'], W['cmp_b2_v'], tk_pad)
    nc = tk_pad // CMP_STRIDE
    c_end = jnp.arange(nc) * CMP_STRIDE + L_CMP - 1
    s = jnp.einsum('bqhgd,bchd->bhgqc', q.astype(F32), kcmp.astype(F32)) * (DB ** -0.5)
    s = s - sl[None, :, :, None, None] * (q_pos[:, None] - c_end[None, :]).astype(F32)
    pc = masked_softmax(s, c_end[None, :] <= q_pos[:, None])
    o_cmp = jnp.einsum('bhgqc,bchd->bqhgd', pc, vcmp.astype(F32)).astype(q.dtype)
    nsb = tk_pad // SEL_BLOCK
    imp = pc.sum(axis=2).reshape(B, KVH, Tq, nsb, SEL_BLOCK // CMP_STRIDE).sum(-1)
    blk = jnp.arange(nsb)[None, :]
    cur = (q_pos // SEL_BLOCK)[:, None]
    forced = (blk == 0) | (blk == cur) | (blk == cur - 1)
    imp = jnp.where(blk > cur, -BIG, imp)
    imp = jnp.where(forced, BIG, imp)
    n_sel = min(N_SEL, nsb)
    _, idx = lax.top_k(imp, n_sel)
    pad = ((0, 0), (0, tk_pad - Tk), (0, 0), (0, 0))
    ksb = jnp.pad(ks, pad).reshape(B, nsb, SEL_BLOCK, KVH, DB).transpose(0, 3, 1, 2, 4)
    vsb = jnp.pad(vs, pad).reshape(B, nsb, SEL_BLOCK, KVH, DB).transpose(0, 3, 1, 2, 4)
    C = SEL_Q_CHUNK if Tq % SEL_Q_CHUNK == 0 else Tq
    nq = Tq // C
    qc = q.reshape(B, nq, C, KVH, GB, DB).swapaxes(0, 1)
    ic = idx.reshape(B, KVH, nq, C, n_sel).transpose(2, 0, 1, 3, 4)
    pcq = q_pos.reshape(nq, C)
    bi = jnp.arange(B)[:, None, None, None]
    hi = jnp.arange(KVH)[None, :, None, None]

    def sel(args):
        qi, ii, qp = args
        kg = ksb[bi, hi, ii].reshape(B, KVH, C, n_sel * SEL_BLOCK, DB)
        vg = vsb[bi, hi, ii].reshape(B, KVH, C, n_sel * SEL_BLOCK, DB)
        kp = (ii[..., None] * SEL_BLOCK + jnp.arange(SEL_BLOCK)).reshape(B, KVH, C, n_sel * SEL_BLOCK)
        dist = qp[None, None, None, :, None] - kp[:, :, None]
        ss = jnp.einsum('bchgd,bhcnd->bhgcn', qi.astype(F32), kg.astype(F32)) * (DB ** -0.5)
        ss = ss - sl[None, :, :, None, None] * dist.astype(F32)
        p = masked_softmax(ss, dist >= 0)
        return jnp.einsum('bhgcn,bhcnd->bchgd', p, vg.astype(F32)).astype(qi.dtype)

    o_sel = lax.map(sel, (qc, ic, pcq)).swapaxes(0, 1).reshape(B, Tq, KVH, GB, DB)
    return o_cmp, o_sel


def attn_prompt(qa, ka, va, qb, kc, vc, ks, vs, kw, vw, *, W, lam, slopes_a, slopes_b):
    T = qa.shape[1]
    pos = jnp.arange(T)
    oa = diff_prompt(qa, ka, va, slopes_a, lam)
    o_cmp, o_sel = nsa_cmp_sel(qb, pos, kc, vc, ks, vs, W, slopes_b)
    o_win = window_prompt(qb, kw, vw, slopes_b)
    return oa, o_cmp, o_sel, o_win


def attn_sample(qa, ka, va, qb, kc, vc, ks, vs, kw, vw, *, W, lam, slopes_a, slopes_b, l, pools, page_table, win_k, win_v):
    cdk, cdv, cck, ccv, csk, csv = pools

    def one_seq(args):
        qa_i, ka_i, va_i, qb_i, kc_i, vc_i, ks_i, vs_i, kw_i, vw_i, wk_i, wv_i, pt_i = args
        past_len = pt_i.shape[0] * PAGE_SIZE
        n_new = qa_i.shape[0]

        def with_past(pool, new):
            rows = pool[l, pt_i]
            rows = rows.reshape((past_len,) + rows.shape[2:]).astype(new.dtype)
            return jnp.concatenate([rows, new], axis=0)[None]

        q_pos = past_len + jnp.arange(n_new)
        k_pos = jnp.arange(past_len + n_new)
        oa = diff_attention(qa_i[None], with_past(cdk, ka_i), with_past(cdv, va_i), q_pos, k_pos, slopes_a, lam)
        o_cmp, o_sel = nsa_cmp_sel(qb_i[None], q_pos, with_past(cck, kc_i), with_past(ccv, vc_i),
                                   with_past(csk, ks_i), with_past(csv, vs_i), W, slopes_b)
        w_buf = wk_i.shape[0]
        kwin = jnp.concatenate([wk_i.astype(kw_i.dtype), kw_i], axis=0)[None]
        vwin = jnp.concatenate([wv_i.astype(vw_i.dtype), vw_i], axis=0)[None]
        w_pos = past_len - w_buf + jnp.arange(w_buf + n_new)
        o_win = gqa_attention(qb_i[None], kwin, vwin, q_pos, w_pos, slopes_b, WINDOW)
        return oa[0], o_cmp[0], o_sel[0], o_win[0]

    return lax.map(one_seq, (qa, ka, va, qb, kc, vc, ks, vs, kw, vw, win_k[l], win_v[l], page_table))


def hier_moe(h, W):
    lg = jnp.einsum('btd,dg->btg', h, W['w_group']).astype(F32) + W['b_group']
    g_idx = jnp.argmax(lg, axis=-1)
    p_grp = jnp.take_along_axis(jax.nn.softmax(lg, axis=-1), g_idx[..., None], axis=-1)
    le = jnp.einsum('btd,dge->btge', h, W['w_router']).astype(F32) + W['b_router']
    le = jnp.take_along_axis(le, g_idx[..., None, None], axis=2)[:, :, 0]
    top_v, top_i = lax.top_k(le, TOP_K)
    w_top = jax.nn.softmax(top_v, axis=-1) * p_grp
    eid = g_idx[..., None] * EXPERTS_PER_GROUP + top_i
    combine = jnp.sum(jax.nn.one_hot(eid, N_EXPERTS, dtype=F32) * w_top[..., None], axis=2)

    def per_seq(args):
        hs, cw = args
        a = jnp.einsum('td,edf->tef', hs, W['w_e_gate'])
        u = jnp.einsum('td,edf->tef', hs, W['w_e_up'])
        z = jax.nn.silu(a) * u * cw[..., None].astype(a.dtype)
        return jnp.einsum('tef,efd->td', z, W['w_e_down'])

    return lax.map(per_seq, (h, combine))


def layer(x, c, W, lambda_init, attn_fn):
    B, T, _ = x.shape
    mod = jnp.einsum('bd,de->be', jax.nn.silu(c), W['w_ada']) + W['b_ada']
    sh1, sc1, g1, sh2, sc2, g2 = [m[:, None, :] for m in jnp.split(mod, 6, axis=-1)]
    h = rms(x, W['norm1_g']) * (1.0 + sc1) + sh1
    p = jnp.einsum('btd,de->bte', h, W['w_in'])
    qa, ka, va, qb, kc, vc, ks, vs, kw, vw, gb, gm = split_proj(p)
    qa = rms(qa.reshape(B, T, HA, 2, DA), W['qn_a'])
    ka = rms(ka.reshape(B, T, HA, 2, DA), W['kn_a'])
    va = va.reshape(B, T, HA, DVA)
    qb = rms(qb.reshape(B, T, KVH, GB, DB), W['qn_b'])
    kc = kc.reshape(B, T, KVH, DB)
    vc = vc.reshape(B, T, KVH, DB)
    ks = rms(ks.reshape(B, T, KVH, DB), W['kn_sel'])
    vs = vs.reshape(B, T, KVH, DB)
    kw = rms(kw.reshape(B, T, KVH, DB), W['kn_win'])
    vw = vw.reshape(B, T, KVH, DB)
    oa, o_cmp, o_sel, o_win = attn_fn(qa, ka, va, qb, kc, vc, ks, vs, kw, vw)
    oa = rms(oa, W['hn_a']) * (1.0 - lambda_init)
    gn = jax.nn.sigmoid(gb.astype(F32)).reshape(B, T, KVH, GB, 3).astype(o_cmp.dtype)
    ob = gn[..., 0:1] * o_cmp + gn[..., 1:2] * o_sel + gn[..., 2:3] * o_win
    ya = jnp.einsum('btk,kd->btd', oa.reshape(B, T, HA * DVA), W['w_up_a'])
    yb = jnp.einsum('btk,kd->btd', ob.reshape(B, T, HB * DB), W['w_up_b'])
    gmg = jax.nn.sigmoid(gm.astype(F32)).reshape(B, T, 2, D_MODEL).astype(ya.dtype)
    mix = jnp.einsum('btd,de->bte', gmg[:, :, 0] * ya + gmg[:, :, 1] * yb, W['w_out'])
    x = x + g1 * mix
    h2 = rms(x, W['norm2_g']) * (1.0 + sc2) + sh2
    x = x + g2 * hier_moe(h2, W)
    return x, (ka, va, kc, vc, ks, vs, kw, vw)


def setup_inputs(seed: int = 0) -> dict:
    key = jax.random.key(seed)
    keys = iter(jax.random.split(key, 80))

    def nrm(shape, scale=1.0):
        return jax.random.normal(next(keys), shape, dtype=jnp.float32) * scale

    def gain(shape):
        return 1.0 + nrm(shape, 0.1)

    n_pages = PAST_LEN // PAGE_SIZE
    n_phys = (DEC_BATCH * n_pages * POOL_NUM) // POOL_DEN
    w_buf = min(WINDOW, PAST_LEN)
    L = DEPTH
    D = D_MODEL
    return {
        'x_prompt': nrm((BATCH, SEQ, D)),
        'x_sample': nrm((DEC_BATCH, DEC_SEQ, D)),
        'cache_diff_k': nrm((L, n_phys, PAGE_SIZE, HA, 2, DA)),
        'cache_diff_v': nrm((L, n_phys, PAGE_SIZE, HA, DVA)),
        'cache_cmp_k': nrm((L, n_phys, PAGE_SIZE, KVH, DB)),
        'cache_cmp_v': nrm((L, n_phys, PAGE_SIZE, KVH, DB)),
        'cache_sel_k': nrm((L, n_phys, PAGE_SIZE, KVH, DB)),
        'cache_sel_v': nrm((L, n_phys, PAGE_SIZE, KVH, DB)),
        'state_win_k': nrm((L, DEC_BATCH, w_buf, KVH, DB)),
        'state_win_v': nrm((L, DEC_BATCH, w_buf, KVH, DB)),
        'page_table': jax.random.permutation(next(keys), n_phys)[:DEC_BATCH * n_pages].reshape(DEC_BATCH, n_pages).astype(jnp.int32),
        'c_prompt': nrm((BATCH, D)),
        'c_sample': nrm((DEC_BATCH, D)),
        'w_ada': nrm((L, D, 6 * D), 0.5 * D ** -0.5),
        'b_ada': nrm((L, 6 * D), 0.01),
        'norm1_g': gain((L, D)),
        'norm2_g': gain((L, D)),
        'w_in': nrm((L, D, N_IN), D ** -0.5),
        'qn_a': gain((L, DA)),
        'kn_a': gain((L, DA)),
        'hn_a': gain((L, DVA)),
        'lam_q1': nrm((L, DA), 0.1),
        'lam_k1': nrm((L, DA), 0.1),
        'lam_q2': nrm((L, DA), 0.1),
        'lam_k2': nrm((L, DA), 0.1),
        'qn_b': gain((L, DB)),
        'kn_cmp': gain((L, DB)),
        'kn_sel': gain((L, DB)),
        'kn_win': gain((L, DB)),
        'cmp_pe_k': nrm((L, L_CMP, DB), 0.1),
        'cmp_w1_k': nrm((L, L_CMP, DB, CMP_HID), (L_CMP * DB) ** -0.5),
        'cmp_b1_k': nrm((L, CMP_HID), 0.01),
        'cmp_w2_k': nrm((L, CMP_HID, DB), CMP_HID ** -0.5),
        'cmp_b2_k': nrm((L, DB), 0.01),
        'cmp_pe_v': nrm((L, L_CMP, DB), 0.1),
        'cmp_w1_v': nrm((L, L_CMP, DB, CMP_HID), (L_CMP * DB) ** -0.5),
        'cmp_b1_v': nrm((L, CMP_HID), 0.01),
        'cmp_w2_v': nrm((L, CMP_HID, DB), CMP_HID ** -0.5),
        'cmp_b2_v': nrm((L, DB), 0.01),
        'w_up_a': nrm((L, HA * DVA, D), (HA * DVA) ** -0.5),
        'w_up_b': nrm((L, HB * DB, D), (HB * DB) ** -0.5),
        'w_out': nrm((L, D, D), D ** -0.5),
        'w_group': nrm((L, D, N_GROUPS), D ** -0.5),
        'b_group': nrm((L, N_GROUPS), 0.01),
        'w_router': nrm((L, D, N_GROUPS, EXPERTS_PER_GROUP), D ** -0.5),
        'b_router': nrm((L, N_GROUPS, EXPERTS_PER_GROUP), 0.01),
        'w_e_gate': nrm((L, N_EXPERTS, D, D_EXPERT), D ** -0.5),
        'w_e_up': nrm((L, N_EXPERTS, D, D_EXPERT), D ** -0.5),
        'w_e_down': nrm((L, N_EXPERTS, D_EXPERT, D), D_EXPERT ** -0.5),
    }


def reference(x_prompt, x_sample, cache_diff_k, cache_diff_v, cache_cmp_k, cache_cmp_v, cache_sel_k, cache_sel_v,
              state_win_k, state_win_v, page_table, c_prompt, c_sample, w_ada, b_ada, norm1_g, norm2_g, w_in,
              qn_a, kn_a, hn_a, lam_q1, lam_k1, lam_q2, lam_k2, qn_b, kn_cmp, kn_sel, kn_win,
              cmp_pe_k, cmp_w1_k, cmp_b1_k, cmp_w2_k, cmp_b2_k, cmp_pe_v, cmp_w1_v, cmp_b1_v, cmp_w2_v, cmp_b2_v,
              w_up_a, w_up_b, w_out, w_group, b_group, w_router, b_router, w_e_gate, w_e_up, w_e_down):
    slopes_a = alibi_slopes(HA)
    slopes_b = alibi_slopes(HB)
    pools = (cache_diff_k, cache_diff_v, cache_cmp_k, cache_cmp_v, cache_sel_k, cache_sel_v)
    xp, xs = x_prompt, x_sample
    new_p = [[] for _ in range(8)]
    new_s = [[] for _ in range(8)]
    for l in range(DEPTH):
        W = {
            'w_ada': w_ada[l], 'b_ada': b_ada[l], 'norm1_g': norm1_g[l], 'norm2_g': norm2_g[l], 'w_in': w_in[l],
            'qn_a': qn_a[l], 'kn_a': kn_a[l], 'hn_a': hn_a[l], 'qn_b': qn_b[l],
            'kn_cmp': kn_cmp[l], 'kn_sel': kn_sel[l], 'kn_win': kn_win[l],
            'cmp_pe_k': cmp_pe_k[l], 'cmp_w1_k': cmp_w1_k[l], 'cmp_b1_k': cmp_b1_k[l], 'cmp_w2_k': cmp_w2_k[l], 'cmp_b2_k': cmp_b2_k[l],
            'cmp_pe_v': cmp_pe_v[l], 'cmp_w1_v': cmp_w1_v[l], 'cmp_b1_v': cmp_b1_v[l], 'cmp_w2_v': cmp_w2_v[l], 'cmp_b2_v': cmp_b2_v[l],
            'w_up_a': w_up_a[l], 'w_up_b': w_up_b[l], 'w_out': w_out[l],
            'w_group': w_group[l], 'b_group': b_group[l], 'w_router': w_router[l], 'b_router': b_router[l],
            'w_e_gate': w_e_gate[l], 'w_e_up': w_e_up[l], 'w_e_down': w_e_down[l],
        }
        lambda_init = 0.8 - 0.6 * math.exp(-0.3 * l)
        lam = (jnp.exp(jnp.sum(lam_q1[l].astype(F32) * lam_k1[l].astype(F32)))
               - jnp.exp(jnp.sum(lam_q2[l].astype(F32) * lam_k2[l].astype(F32))) + lambda_init)
        fp = functools.partial(attn_prompt, W=W, lam=lam, slopes_a=slopes_a, slopes_b=slopes_b)
        xp, rp = layer(xp, c_prompt, W, lambda_init, fp)
        fs = functools.partial(attn_sample, W=W, lam=lam, slopes_a=slopes_a, slopes_b=slopes_b, l=l, pools=pools,
                               page_table=page_table, win_k=state_win_k, win_v=state_win_v)
        xs, rs = layer(xs, c_sample, W, lambda_init, fs)
        wp = min(WINDOW, xp.shape[1])
        wb = state_win_k.shape[2]
        win_pk = rp[6][:, -wp:]
        win_pv = rp[7][:, -wp:]
        win_sk = jnp.concatenate([state_win_k[l].astype(rs[6].dtype), rs[6]], axis=1)[:, -wb:]
        win_sv = jnp.concatenate([state_win_v[l].astype(rs[7].dtype), rs[7]], axis=1)[:, -wb:]
        rows_p = list(rp[:6]) + [win_pk, win_pv]
        rows_s = list(rs[:6]) + [win_sk, win_sv]
        for i in range(8):
            new_p[i].append(rows_p[i])
            new_s[i].append(rows_s[i])
    p_diff_k, p_diff_v, p_cmp_k, p_cmp_v, p_sel_k, p_sel_v, p_win_k, p_win_v = [jnp.stack(a) for a in new_p]
    s_diff_k, s_diff_v, s_cmp_k, s_cmp_v, s_sel_k, s_sel_v, s_win_k, s_win_v = [jnp.stack(a) for a in new_s]
    return (xp, xs, p_diff_k, p_diff_v, p_cmp_k, p_cmp_v, p_sel_k, p_sel_v, p_win_k, p_win_v,
            s_diff_k, s_diff_v, s_cmp_k, s_cmp_v, s_sel_k, s_sel_v, s_win_k, s_win_v)
```

```python
import functools
import math

import jax
import jax.numpy as jnp
from jax import lax
from jax.experimental import pallas as pl
from jax.experimental.pallas import tpu as pltpu

F32 = jnp.float32
BF16 = jnp.bfloat16

HA, DA, DVA = 4, 64, 128
HB, KVH, GB, DB = 8, 2, 4, 64
L_CMP, CMP_STRIDE, CMP_HID = 32, 16, 128
SEL_BLOCK, N_SEL, WINDOW = 64, 8, 512
N_GROUPS, EXPERTS_PER_GROUP, N_EXPERTS, D_EXPERT = 4, 4, 16, 256
RMS_EPS = 1e-6
NEG_INF = -1e30
BIG = 1e9
REMOVED = -3e38
SLOPES_A = tuple(2.0 ** (-8.0 * (i + 1) / HA) for i in range(HA))
SLOPES_B = tuple(2.0 ** (-8.0 * (i + 1) / HB) for i in range(HB))
QK_SCALE = 0.125

LANES = 128
VMEM_LIMIT = 56 * 1024 * 1024

ROW_TILE = 256
MOE_ROW_TILE = 512
DIFF_TQ = 256
NSA_TQ = 128
NSA_TK = 256
CMP_TQ = 256
PAGES_PER_STEP = 8


def _cparams(sem):
    return pltpu.CompilerParams(dimension_semantics=sem, vmem_limit_bytes=VMEM_LIMIT)


def _const_spec(shape):
    nd = len(shape)
    return pl.BlockSpec(shape, lambda *_: (0,) * nd)


def _round_up(x, m):
    return -(-x // m) * m


def _split3(x):
    x1 = x.astype(BF16)
    r = x - x1.astype(F32)
    x2 = r.astype(BF16)
    x3 = (r - x2.astype(F32)).astype(BF16)
    return x1, x2, x3


def _dot(a, b):
    return jnp.dot(a, b, preferred_element_type=F32)


def _dot_nt(a, b):
    return lax.dot_general(a, b, (((1,), (1,)), ((), ())), preferred_element_type=F32)


def _gelu_tanh(x):
    return 0.5 * x * (1.0 + jnp.tanh(0.7978845608028654 * (x + 0.044715 * x * x * x)))


def _topk_mask(imp, k):
    width = imp.shape[-1]
    lane = lax.broadcasted_iota(jnp.int32, imp.shape, 1).astype(F32)
    sel = jnp.zeros(imp.shape, F32)
    work = imp
    for _ in range(k):
        m = jnp.max(work, axis=-1, keepdims=True)
        idx = jnp.min(jnp.where(work == m, lane, float(width)), axis=-1, keepdims=True)
        hit = lane == idx
        sel = jnp.where(hit, 1.0, sel)
        work = jnp.where(hit, REMOVED, work)
    return sel


def _ada_kernel(c_ref, w_ref, b_ref, q1_ref, k1_ref, q2_ref, k2_ref, o_ref, lam_ref, *, lambda_init):
    c = c_ref[...]
    s = c * jax.nn.sigmoid(c)
    s1, s2, s3 = _split3(s)
    w1, w2, w3 = _split3(w_ref[...])
    acc = _dot(s1, w1) + (_dot(s1, w2) + _dot(s2, w1)) + (_dot(s1, w3) + _dot(s2, w2) + _dot(s3, w1))
    o_ref[...] = acc + b_ref[...]
    a = jnp.sum(q1_ref[...] * k1_ref[...], axis=-1, keepdims=True)
    b = jnp.sum(q2_ref[...] * k2_ref[...], axis=-1, keepdims=True)
    lam_ref[...] = jnp.broadcast_to(jnp.exp(a) - jnp.exp(b) + lambda_init, lam_ref.shape)


def _ada(c_all, w_ada, b_ada, lq1, lk1, lq2, lk2, lambda_init):
    n, d = c_all.shape
    n_out = w_ada.shape[1]
    tn = 512
    vec = pl.BlockSpec((1, DA), lambda j: (0, 0))
    return pl.pallas_call(
        functools.partial(_ada_kernel, lambda_init=lambda_init),
        out_shape=(jax.ShapeDtypeStruct((n, n_out), F32), jax.ShapeDtypeStruct((1, LANES), F32)),
        grid=(n_out // tn,),
        in_specs=[pl.BlockSpec((n, d), lambda j: (0, 0)), pl.BlockSpec((d, tn), lambda j: (0, j)),
                  pl.BlockSpec((1, tn), lambda j: (0, j)), vec, vec, vec, vec],
        out_specs=(pl.BlockSpec((n, tn), lambda j: (0, j)), pl.BlockSpec((1, LANES), lambda j: (0, 0))),
        compiler_params=_cparams(("arbitrary",)),
        name="ada",
    )(c_all, w_ada, b_ada.reshape(1, -1), lq1.reshape(1, -1), lk1.reshape(1, -1), lq2.reshape(1, -1), lk2.reshape(1, -1))


_W_QA, _W_KA, _W_VA, _W_QB = 0, 512, 1024, 1536
_W_KV6 = _W_QB + HB * LANES
_W_GB = _W_KV6 + 6 * 128
_W_GM = _W_GB + 128
_W_COLS = _W_GM + 2048


def _group_norm(p, gain, ones):
    n = p.shape[1]
    w = 256 if n % 256 == 0 else 128
    sq = (p * p).astype(BF16)
    parts = [_dot(sq[:, c:c + w], ones[:w, :w]) for c in range(0, n, w)]
    ss = parts[0] if len(parts) == 1 else jnp.concatenate(parts, axis=1)
    return p * lax.rsqrt(ss * (1.0 / 64.0) + RMS_EPS) * gain


def _proj_kernel(x_ref, sc_ref, sh_ref, ng_ref, w_ref, gqa_ref, gka_ref, gqb_ref, gks_ref, gkw_ref, ones_ref,
                 qa_ref, ka_ref, va_ref, qb_ref, kc_ref, vc_ref, ks_ref, vs_ref, kw_ref, vw_ref, gb_ref, gm_ref):
    x = x_ref[...]
    h = x * lax.rsqrt(jnp.mean(x * x, axis=-1, keepdims=True) + RMS_EPS) * ng_ref[...]
    h = h * (1.0 + sc_ref[...]) + sh_ref[...]
    hb = h.astype(BF16)
    ones = ones_ref[...]

    def seg(a, n):
        return _dot(hb, w_ref[:, a:a + n])

    qa_ref[...] = (_group_norm(seg(_W_QA, 512), gqa_ref[...], ones) * QK_SCALE).astype(BF16)
    ka_ref[...] = _group_norm(seg(_W_KA, 512), gka_ref[...], ones)
    va_ref[...] = seg(_W_VA, 512)
    qb_ref[...] = (_group_norm(seg(_W_QB, 1024), gqb_ref[...], ones) * QK_SCALE).astype(BF16)
    kc_ref[...] = seg(_W_KV6, 128)
    vc_ref[...] = seg(_W_KV6 + 128, 128)
    ks_ref[...] = _group_norm(seg(_W_KV6 + 256, 128), gks_ref[...], ones)
    vs_ref[...] = seg(_W_KV6 + 384, 128)
    kw_ref[...] = _group_norm(seg(_W_KV6 + 512, 128), gkw_ref[...], ones)
    vw_ref[...] = seg(_W_KV6 + 640, 128)
    gb_ref[...] = jax.nn.sigmoid(seg(_W_GB, 128))
    gm_ref[:, :1024] = jax.nn.sigmoid(seg(_W_GM, 1024)).astype(BF16)
    gm_ref[:, 1024:] = jax.nn.sigmoid(seg(_W_GM + 1024, 1024)).astype(BF16)


def _mod_spec(per_token, tm, bps, d):
    if per_token:
        return pl.BlockSpec((tm, d), lambda i, *_: (i, 0))
    return pl.BlockSpec((None, 1, d), lambda i, *_: (i // bps, 0, 0))


def _proj(x, sc, sh, per_token, rows_per_seq, wp):
    n, d = x.shape
    tm = min(ROW_TILE, n)
    bps = max(rows_per_seq // tm, 1)
    mod = _mod_spec(per_token, tm, bps, d)

    def row(width):
        return pl.BlockSpec((tm, width), lambda i: (i, 0))

    widths = [(512, BF16), (512, F32), (512, F32), (1024, BF16)] + [(128, F32)] * 6 + [(128, F32), (2048, BF16)]
    return pl.pallas_call(
        _proj_kernel,
        out_shape=tuple(jax.ShapeDtypeStruct((n, w), dt) for w, dt in widths),
        grid=(n // tm,),
        in_specs=[row(d), mod, mod, _const_spec((1, d)), _const_spec((d, _W_COLS)),
                  _const_spec((1, 512)), _const_spec((1, 512)), _const_spec((1, 1024)),
                  _const_spec((1, 128)), _const_spec((1, 128)), _const_spec((256, 256))],
        out_specs=tuple(row(w) for w, _ in widths),
        compiler_params=_cparams(("parallel",)),
        name="proj",
    )(x, sc, sh, wp["norm1_g"], wp["w_in"], wp["g_qa"], wp["g_ka"], wp["g_qb"], wp["g_ks"], wp["g_kw"], wp["ones"])


def _flash_step(m, acc, s, valid, v_ext):
    if valid is not None:
        s = jnp.where(valid, s, NEG_INF)
    m_new = jnp.maximum(m, jnp.max(s, axis=-1, keepdims=True))
    alpha = jnp.exp(m - m_new)
    p = jnp.exp(s - m_new)
    if valid is not None:
        p = jnp.where(valid, p, 0.0)
    acc = alpha * acc + _dot(p.astype(BF16), v_ext)
    return m_new, acc


def _diff_prompt_kernel(slope_ref, lam_ref, q_ref, k_ref, v_ref, hn_ref, o_ref, *, tq, out_scale):
    h = pl.program_id(1)
    i = pl.program_id(2)
    slope = slope_ref[h]
    lam = lam_ref[0]
    q = q_ref[...]
    lane = lax.broadcasted_iota(jnp.int32, q.shape, 1)
    zero = jnp.zeros_like(q)
    qmaps = (jnp.where(lane < DA, q, zero), jnp.where(lane >= DA, q, zero))
    r = lax.broadcasted_iota(jnp.int32, (tq, tq), 0)
    c = lax.broadcasted_iota(jnp.int32, (tq, tq), 1)
    causal = r >= c
    bias = slope * (r - c).astype(F32)
    ones = jnp.ones((tq, DVA), BF16)

    def block(j, carry, valid):
        start = pl.multiple_of(j * tq, tq)
        kb = k_ref[pl.ds(start, tq), :].astype(BF16)
        v_ext = jnp.concatenate([v_ref[pl.ds(start, tq), :].astype(BF16), ones], axis=1)
        off = slope * ((i - j) * tq).astype(F32)
        out = []
        for cmap in range(2):
            m, acc = carry[cmap]
            s = _dot_nt(qmaps[cmap], kb) - bias - off
            out.append(_flash_step(m, acc, s, valid, v_ext))
        return tuple(out)

    init = tuple((jnp.full((tq, 1), NEG_INF, F32), jnp.zeros((tq, 2 * DVA), F32)) for _ in range(2))
    carry = lax.fori_loop(0, i, lambda j, cr: block(j, cr, None), init)
    carry = block(i, carry, causal)
    outs = [acc[:, :DVA] / acc[:, DVA:] for _, acc in carry]
    o = outs[0] - lam * outs[1]
    o = o * lax.rsqrt(jnp.mean(o * o, axis=-1, keepdims=True) + RMS_EPS) * hn_ref[...]
    o_ref[...] = (o * out_scale).astype(BF16)


def _diff_prompt(qa, ka, va, slopes, lam, hn, out_scale):
    b, t, _ = qa.shape
    tq = min(DIFF_TQ, t)
    smem = pl.BlockSpec(memory_space=pltpu.SMEM)
    return pl.pallas_call(
        functools.partial(_diff_prompt_kernel, tq=tq, out_scale=out_scale),
        out_shape=jax.ShapeDtypeStruct((b, t, HA * DVA), BF16),
        grid=(b, HA, t // tq),
        in_specs=[smem, smem,
                  pl.BlockSpec((None, tq, 2 * DA), lambda bi, h, i: (bi, i, h)),
                  pl.BlockSpec((None, t, 2 * DA), lambda bi, h, i: (bi, 0, h)),
                  pl.BlockSpec((None, t, DVA), lambda bi, h, i: (bi, 0, h)),
                  pl.BlockSpec((1, DVA), lambda bi, h, i: (0, 0))],
        out_specs=pl.BlockSpec((None, tq, DVA), lambda bi, h, i: (bi, i, h)),
        compiler_params=_cparams(("parallel", "parallel", "arbitrary")),
        name="diff_prompt",
    )(slopes, lam, qa, ka, va, hn)


def _compress(buf_ref, m_rows, w1_ref, pe_ref, b1_ref, w2_ref, b2_ref):
    acc = jnp.zeros((m_rows, KVH * CMP_HID), F32)
    group = 8
    for jg in range(L_CMP // group):
        cols = slice(jg * group * LANES, (jg + 1) * group * LANES)
        xs = [buf_ref[pl.ds(jg * group + jj, m_rows, stride=CMP_STRIDE), :] for jj in range(group)]
        x = jnp.concatenate(xs, axis=1) + pe_ref[:, cols]
        acc = acc + _dot(x.astype(BF16), w1_ref[cols, :])
    hid = _gelu_tanh(acc + b1_ref[...])
    return _dot(hid.astype(BF16), w2_ref[...]) + b2_ref[...]


def _cmp_attention(q_ext, kcmp_b, vcmp_b, qpos, slopes_h, gsum, nsb, n_q_rows):
    m_rows = kcmp_b.shape[0]
    cend = (lax.broadcasted_iota(jnp.int32, (1, m_rows), 1) * CMP_STRIDE + (L_CMP - 1))
    visible = cend <= qpos
    dist = (qpos - cend).astype(F32)
    outs = []
    psum = jnp.zeros((n_q_rows, m_rows), F32)
    for g in range(GB):
        s = _dot_nt(q_ext[g], kcmp_b) - slopes_h[g] * dist
        s = jnp.where(visible, s, NEG_INF)
        mx = jnp.max(s, axis=-1, keepdims=True)
        e = jnp.where(visible, jnp.exp(s - mx), 0.0)
        p = e / jnp.maximum(jnp.sum(e, axis=-1, keepdims=True), 1e-30)
        outs.append(_dot(p.astype(BF16), vcmp_b))
        psum = psum + p
    p1, p2, p3 = _split3(psum)
    imp = _dot(p1, gsum) + _dot(p2, gsum) + _dot(p3, gsum)
    blk = lax.broadcasted_iota(jnp.int32, (1, imp.shape[1]), 1)
    cur = qpos // SEL_BLOCK
    forced = (blk == 0) | (blk == cur) | (blk == cur - 1)
    imp = jnp.where(blk > cur, -BIG, imp)
    imp = jnp.where(forced, BIG, imp)
    imp = jnp.where(blk >= nsb, REMOVED, imp)
    return outs, _topk_mask(imp, min(N_SEL, nsb))


def _cmp_prompt_kernel(q_ref, kc_ref, vc_ref, w1k_ref, pek_ref, b1k_ref, w2k_ref, b2k_ref,
                       w1v_ref, pev_ref, b1v_ref, w2v_ref, b2v_ref, kn_ref, ones_ref, gsum_ref,
                       o_ref, sel_ref, bufk, bufv, kcmp, vcmp, *, t, tq, m_rows, nsb):
    i = pl.program_id(1)

    @pl.when(i == 0)
    def _():
        for buf, src in ((bufk, kc_ref), (bufv, vc_ref)):
            buf[pl.ds(0, t), :] = src[...]
            buf[pl.ds(t, buf.shape[0] - t), :] = jnp.zeros((buf.shape[0] - t, LANES), F32)
        k = _compress(bufk, m_rows, w1k_ref, pek_ref, b1k_ref, w2k_ref, b2k_ref)
        kcmp[...] = _group_norm(k, kn_ref[...], ones_ref[...]).astype(BF16)
        vcmp[...] = _compress(bufv, m_rows, w1v_ref, pev_ref, b1v_ref, w2v_ref, b2v_ref).astype(BF16)

    qpos = i * tq + lax.broadcasted_iota(jnp.int32, (tq, 1), 0)
    kcmp_b = kcmp[...]
    vcmp_b = vcmp[...]
    gsum = gsum_ref[...]
    for kvh in range(KVH):
        q_ext = [q_ref[:, (kvh * GB + g) * LANES:(kvh * GB + g + 1) * LANES] for g in range(GB)]
        outs, sel = _cmp_attention(q_ext, kcmp_b, vcmp_b, qpos, SLOPES_B[kvh * GB:(kvh + 1) * GB], gsum, nsb, tq)
        for g in range(GB):
            o_ref[:, (kvh * GB + g) * LANES:(kvh * GB + g + 1) * LANES] = outs[g].astype(BF16)
        sel_ref[kvh] = sel.astype(BF16)


def _cmp_prompt(qb_ext, kc, vc, wp):
    b, t, _ = qb_ext.shape
    tq = min(CMP_TQ, t)
    nc = t // CMP_STRIDE
    m_rows = _round_up(nc, LANES)
    nsb = t // SEL_BLOCK
    nsbp = _round_up(nsb, LANES)
    buf_rows = CMP_STRIDE * (m_rows - 1) + L_CMP
    gsum = ((jnp.arange(m_rows)[:, None] // (SEL_BLOCK // CMP_STRIDE) == jnp.arange(nsbp)[None, :])
            & (jnp.arange(m_rows)[:, None] < nc)).astype(BF16)
    kv = pl.BlockSpec((None, t, LANES), lambda bi, i: (bi, 0, 0))
    cw = [_const_spec((L_CMP * LANES, KVH * CMP_HID)), _const_spec((1, L_CMP * LANES)), _const_spec((1, KVH * CMP_HID)),
          _const_spec((KVH * CMP_HID, LANES)), _const_spec((1, LANES))]
    return pl.pallas_call(
        functools.partial(_cmp_prompt_kernel, t=t, tq=tq, m_rows=m_rows, nsb=nsb),
        out_shape=(jax.ShapeDtypeStruct((b, t, HB * LANES), BF16), jax.ShapeDtypeStruct((b, KVH, t, nsbp), BF16)),
        grid=(b, t // tq),
        in_specs=[pl.BlockSpec((None, tq, HB * LANES), lambda bi, i: (bi, i, 0)), kv, kv] + cw + cw
                 + [_const_spec((1, LANES)), _const_spec((256, 256)), _const_spec((m_rows, nsbp))],
        out_specs=(pl.BlockSpec((None, tq, HB * LANES), lambda bi, i: (bi, i, 0)),
                   pl.BlockSpec((None, KVH, tq, nsbp), lambda bi, i: (bi, 0, i, 0))),
        scratch_shapes=[pltpu.VMEM((buf_rows, LANES), F32), pltpu.VMEM((buf_rows, LANES), F32),
                        pltpu.VMEM((m_rows, LANES), BF16), pltpu.VMEM((m_rows, LANES), BF16)],
        compiler_params=_cparams(("parallel", "arbitrary")),
        name="cmp_prompt",
    )(qb_ext, kc, vc, *wp["cmp_k"], *wp["cmp_v"], wp["g_kc"], wp["ones"], gsum)


def _compact_pairs(ext, kvh):
    lane = lax.broadcasted_iota(jnp.int32, ext[0].shape, 1)
    pieces = []
    for pair in range(GB // 2):
        even, odd = ext[2 * pair], ext[2 * pair + 1]
        if kvh == 0:
            pieces.append(jnp.where(lane < DB, even, pltpu.roll(odd, DB, 1)))
        else:
            pieces.append(jnp.where(lane < DB, pltpu.roll(even, DB, 1), odd))
    return jnp.concatenate(pieces, axis=1)


def _nsa_prompt_kernel(q_ref, ks_ref, vs_ref, kw_ref, vw_ref, sel_ref, ocmp_ref, gb_ref, e_ref, o_ref, *, tq, tk):
    i = pl.program_id(1)
    rows = GB * tq
    r = lax.broadcasted_iota(jnp.int32, (rows, tk), 0) % tq
    c = lax.broadcasted_iota(jnp.int32, (rows, tk), 1)
    dbase = r - c
    gidx = lax.broadcasted_iota(jnp.int32, (rows, 1), 0) // tq
    lane = lax.broadcasted_iota(jnp.int32, (tk, LANES), 1)
    gates = gb_ref[...]
    j_hi = (i * tq + tq - 1) // tk
    j_lo_win = jnp.maximum(i * tq - (WINDOW - 1), 0) // tk

    for kvh in range(KVH):
        slope_col = jnp.zeros((rows, 1), F32)
        for g in range(GB):
            slope_col = jnp.where(gidx == g, SLOPES_B[kvh * GB + g], slope_col)
        bias = slope_col * dbase.astype(F32)
        q4 = jnp.concatenate([q_ref[:, (kvh * GB + g) * LANES:(kvh * GB + g + 1) * LANES] for g in range(GB)], axis=0)
        selm = sel_ref[kvh]
        own = (lane >= kvh * DB) & (lane < (kvh + 1) * DB)

        def scores(k_ref, v_ref, j):
            start = pl.multiple_of(j * tk, tk)
            kb = k_ref[pl.ds(start, tk), :].astype(BF16)
            v_ext = jnp.where(own, v_ref[pl.ds(start, tk), :], 1.0).astype(BF16)
            shift = j * tk - i * tq
            s = _dot_nt(q4, kb) - bias + slope_col * shift.astype(F32)
            return s, v_ext, shift, start

        def sel_block(j, carry):
            s, v_ext, shift, start = scores(ks_ref, vs_ref, j)
            mq = _dot(selm, e_ref[:, pl.ds(start, tk)])
            m4 = jnp.concatenate([mq] * GB, axis=0)
            valid = (m4 > 0.5) & (dbase >= shift)
            return _flash_step(carry[0], carry[1], s, valid, v_ext)

        def win_block(j, carry):
            s, v_ext, shift, _ = scores(kw_ref, vw_ref, j)
            dist = dbase - shift
            valid = (dist >= 0) & (dist < WINDOW)
            return _flash_step(carry[0], carry[1], s, valid, v_ext)

        init = (jnp.full((rows, 1), NEG_INF, F32), jnp.zeros((rows, LANES), F32))
        _, acc_s = lax.fori_loop(0, j_hi + 1, sel_block, init)
        _, acc_w = lax.fori_loop(j_lo_win, j_hi + 1, win_block, init)

        def finish(acc):
            den = pltpu.roll(acc, DB, 1)
            return acc / jnp.maximum(den, 1e-30)

        o_s, o_w = finish(acc_s), finish(acc_w)
        ext = []
        for g in range(GB):
            head = kvh * GB + g
            col = kvh * GB * 3 + g * 3
            oc = ocmp_ref[:, head * LANES:(head + 1) * LANES].astype(F32)
            ext.append(gates[:, col:col + 1] * oc + gates[:, col + 1:col + 2] * o_s[g * tq:(g + 1) * tq]
                       + gates[:, col + 2:col + 3] * o_w[g * tq:(g + 1) * tq])
        o_ref[:, kvh * GB * DB:(kvh + 1) * GB * DB] = _compact_pairs(ext, kvh).astype(BF16)


def _nsa_prompt(qb_ext, ks, vs, kw, vw, sel, ocmp, gbs):
    b, t, _ = qb_ext.shape
    tq = min(NSA_TQ, t)
    tk = min(NSA_TK, t)
    nsbp = sel.shape[-1]
    emat = (jnp.arange(t)[None, :] // SEL_BLOCK == jnp.arange(nsbp)[:, None]).astype(BF16)
    kv = pl.BlockSpec((None, t, LANES), lambda bi, i: (bi, 0, 0))
    return pl.pallas_call(
        functools.partial(_nsa_prompt_kernel, tq=tq, tk=tk),
        out_shape=jax.ShapeDtypeStruct((b, t, HB * DB), BF16),
        grid=(b, t // tq),
        in_specs=[pl.BlockSpec((None, tq, HB * LANES), lambda bi, i: (bi, i, 0)), kv, kv, kv, kv,
                  pl.BlockSpec((None, KVH, tq, nsbp), lambda bi, i: (bi, 0, i, 0)),
                  pl.BlockSpec((None, tq, HB * LANES), lambda bi, i: (bi, i, 0)),
                  pl.BlockSpec((None, tq, LANES), lambda bi, i: (bi, i, 0)),
                  _const_spec((nsbp, t))],
        out_specs=pl.BlockSpec((None, tq, HB * DB), lambda bi, i: (bi, i, 0)),
        compiler_params=_cparams(("parallel", "arbitrary")),
        name="nsa_prompt",
    )(qb_ext, ks, vs, kw, vw, sel, ocmp, gbs, emat)


def _merge_kernel(oa_ref, ob_ref, gm_ref, x_ref, g1_ref, sc_ref, sh_ref, ng_ref, wa_ref, wb_ref, wo_ref,
                  wrh_ref, wrl_ref, br_ref, x1_ref, h2_ref, cw_ref):
    ya = _dot(oa_ref[...], wa_ref[...])
    yb = _dot(ob_ref[...], wb_ref[...])
    d = ya.shape[1]
    t = gm_ref[:, :d].astype(F32) * ya + gm_ref[:, d:].astype(F32) * yb
    mix = _dot(t.astype(BF16), wo_ref[...])
    x1 = x_ref[...] + g1_ref[...] * mix
    x1_ref[...] = x1
    h2 = x1 * lax.rsqrt(jnp.mean(x1 * x1, axis=-1, keepdims=True) + RMS_EPS) * ng_ref[...]
    h2 = h2 * (1.0 + sc_ref[...]) + sh_ref[...]
    h2_ref[...] = h2.astype(BF16)
    hh = h2.astype(BF16)
    hl = (h2 - hh.astype(F32)).astype(BF16)
    logits = _dot(hh, wrh_ref[...]) + (_dot(hh, wrl_ref[...]) + _dot(hl, wrh_ref[...])) + br_ref[...]
    lane = lax.broadcasted_iota(jnp.int32, logits.shape, 1)
    lane_f = lane.astype(F32)
    is_grp = lane < N_GROUPS
    lg = jnp.where(is_grp, logits, NEG_INF)
    gmax = jnp.max(lg, axis=-1, keepdims=True)
    gidx = jnp.min(jnp.where(lg == gmax, lane_f, float(LANES)), axis=-1, keepdims=True)
    p_grp = 1.0 / jnp.sum(jnp.where(is_grp, jnp.exp(lg - gmax), 0.0), axis=-1, keepdims=True)
    egrp = ((lane - N_GROUPS) // EXPERTS_PER_GROUP).astype(F32)
    in_grp = (lane >= N_GROUPS) & (lane < N_GROUPS + N_EXPERTS) & (egrp == gidx)
    le = jnp.where(in_grp, logits, REMOVED)
    v1 = jnp.max(le, axis=-1, keepdims=True)
    i1 = jnp.min(jnp.where(le == v1, lane_f, float(LANES)), axis=-1, keepdims=True)
    le2 = jnp.where(lane_f == i1, REMOVED, le)
    v2 = jnp.max(le2, axis=-1, keepdims=True)
    i2 = jnp.min(jnp.where(le2 == v2, lane_f, float(LANES)), axis=-1, keepdims=True)
    e2 = jnp.exp(v2 - v1)
    w1 = 1.0 / (1.0 + e2)
    cw_ref[...] = jnp.where(lane_f == i1, w1 * p_grp, jnp.where(lane_f == i2, e2 * w1 * p_grp, 0.0))


def _merge(oa, ob, gm, x, g1, sc2, sh2, per_token, rows_per_seq, wp):
    n, d = x.shape
    tm = min(ROW_TILE, n)
    bps = max(rows_per_seq // tm, 1)
    mod = _mod_spec(per_token, tm, bps, d)

    def row(width):
        return pl.BlockSpec((tm, width), lambda i: (i, 0))

    return pl.pallas_call(
        _merge_kernel,
        out_shape=(jax.ShapeDtypeStruct((n, d), F32), jax.ShapeDtypeStruct((n, d), BF16),
                   jax.ShapeDtypeStruct((n, LANES), F32)),
        grid=(n // tm,),
        in_specs=[row(HA * DVA), row(HB * DB), row(2 * d), row(d), mod, mod, mod, _const_spec((1, d)),
                  _const_spec((HA * DVA, d)), _const_spec((HB * DB, d)), _const_spec((d, d)),
                  _const_spec((d, LANES)), _const_spec((d, LANES)), _const_spec((1, LANES))],
        out_specs=(row(d), row(d), row(LANES)),
        compiler_params=_cparams(("parallel",)),
        name="merge",
    )(oa, ob, gm, x, g1, sc2, sh2, wp["norm2_g"], wp["w_up_a"], wp["w_up_b"], wp["w_out"],
      wp["w_r_hi"], wp["w_r_lo"], wp["b_r"])


def _moe_kernel(h_ref, cw_ref, x_ref, g2_ref, wgu_ref, wd_ref, o_ref, acc_ref):
    e = pl.program_id(1)

    @pl.when(e == 0)
    def _():
        acc_ref[...] = jnp.zeros_like(acc_ref)

    lane = lax.broadcasted_iota(jnp.int32, cw_ref.shape, 1)
    cw = jnp.sum(jnp.where(lane == e + N_GROUPS, cw_ref[...], 0.0), axis=-1, keepdims=True)
    au = _dot(h_ref[...], wgu_ref[...])
    a = au[:, :D_EXPERT]
    z = a * jax.nn.sigmoid(a) * au[:, D_EXPERT:] * cw
    acc_ref[...] += _dot(z.astype(BF16), wd_ref[...])

    @pl.when(e == pl.num_programs(1) - 1)
    def _():
        o_ref[...] = x_ref[...] + g2_ref[...] * acc_ref[...]


def _moe(h2, cw, x1, g2, per_token, rows_per_seq, wp):
    n, d = x1.shape
    tm = min(MOE_ROW_TILE, n)
    bps = max(rows_per_seq // tm, 1)
    if per_token:
        mod = pl.BlockSpec((tm, d), lambda i, e: (i, 0))
    else:
        mod = pl.BlockSpec((None, 1, d), lambda i, e: (i // bps, 0, 0))
    return pl.pallas_call(
        _moe_kernel,
        out_shape=jax.ShapeDtypeStruct((n, d), F32),
        grid=(n // tm, N_EXPERTS),
        in_specs=[pl.BlockSpec((tm, d), lambda i, e: (i, 0)), pl.BlockSpec((tm, LANES), lambda i, e: (i, 0)),
                  pl.BlockSpec((tm, d), lambda i, e: (i, 0)), mod,
                  pl.BlockSpec((None, d, 2 * D_EXPERT), lambda i, e: (e, 0, 0)),
                  pl.BlockSpec((None, D_EXPERT, d), lambda i, e: (e, 0, 0))],
        out_specs=pl.BlockSpec((tm, d), lambda i, e: (i, 0)),
        scratch_shapes=[pltpu.VMEM((tm, d), F32)],
        compiler_params=_cparams(("parallel", "arbitrary")),
        name="moe",
    )(h2, cw, x1, g2, wp["w_gu"], wp["w_down"])


def _page_specs(n_pages, ppc, width, page):
    def spec(p):
        return pl.BlockSpec((None, page, width), lambda s, c, pt: (pt[s * n_pages + c * ppc + p], 0, 0))
    return [spec(p) for p in range(ppc)]


def _pad_rows(x, rows):
    return jnp.concatenate([x, jnp.zeros((rows - x.shape[0], x.shape[1]), x.dtype)], axis=0)


def _diff_sample_kernel(pt_ref, lam_ref, q_ref, kn_ref, vn_ref, hn_ref, *rest, ppc, page, past_len, n_new, out_scale):
    k_refs, v_refs = rest[:ppc], rest[ppc:2 * ppc]
    o_ref, m_ref, acc_ref = rest[2 * ppc:]
    c = pl.program_id(1)
    rows = HA * 2 * 8
    lam = lam_ref[0]

    @pl.when(c == 0)
    def _():
        m_ref[...] = jnp.full(m_ref.shape, NEG_INF, F32)
        acc_ref[...] = jnp.zeros_like(acc_ref)

    ridx = lax.broadcasted_iota(jnp.int32, (rows, 1), 0)
    qpos = past_len + ridx % 8
    slope = jnp.zeros((rows, 1), F32)
    for h in range(HA):
        slope = jnp.where(ridx // 16 == h, SLOPES_A[h], slope)
    q = q_ref[...]

    def update(kb, vb, kpos, valid):
        s = _dot_nt(q, kb) - slope * (qpos - kpos).astype(F32)
        v_ext = jnp.concatenate([vb, jnp.ones((vb.shape[0], LANES), BF16)], axis=1)
        m_new, acc = _flash_step(m_ref[...], acc_ref[...], s, valid, v_ext)
        m_ref[...] = m_new
        acc_ref[...] = acc

    chunk = ppc * page
    kb = jnp.concatenate([r[...] for r in k_refs], axis=0).astype(BF16)
    vb = jnp.concatenate([r[...] for r in v_refs], axis=0).astype(BF16)
    kpos = c * chunk + lax.broadcasted_iota(jnp.int32, (1, chunk), 1)
    update(kb, vb, kpos, None)

    @pl.when(c == pl.num_programs(1) - 1)
    def _():
        kn = _pad_rows(kn_ref[...], LANES).astype(BF16)
        vn = _pad_rows(vn_ref[...], LANES).astype(BF16)
        npos = past_len + lax.broadcasted_iota(jnp.int32, (1, LANES), 1)
        update(kn, vn, npos, (npos <= qpos) & (npos < past_len + n_new))
        acc = acc_ref[...]
        den = acc[:, HA * DVA:HA * DVA + LANES]
        for h in range(HA):
            r0 = h * 16
            o0 = acc[r0:r0 + 8, h * DVA:(h + 1) * DVA] / den[r0:r0 + 8]
            o1 = acc[r0 + 8:r0 + 16, h * DVA:(h + 1) * DVA] / den[r0 + 8:r0 + 16]
            o = o0 - lam * o1
            o = o * lax.rsqrt(jnp.mean(o * o, axis=-1, keepdims=True) + RMS_EPS) * hn_ref[...]
            o_ref[:, h * DVA:(h + 1) * DVA] = (o * out_scale).astype(BF16)


def _diff_sample(pt, lam, qbd, k_new, v_new, hn, pool_k, pool_v, n_pages, past_len, n_new, out_scale):
    s = qbd.shape[0]
    page = pool_k.shape[1]
    ppc = math.gcd(PAGES_PER_STEP, n_pages)
    rows = HA * 2 * 8
    per_seq = lambda shape: pl.BlockSpec((None,) + shape, lambda si, c, pt: (si, 0, 0))
    return pl.pallas_call(
        functools.partial(_diff_sample_kernel, ppc=ppc, page=page, past_len=past_len, n_new=n_new, out_scale=out_scale),
        out_shape=jax.ShapeDtypeStruct((s, 8, HA * DVA), BF16),
        grid_spec=pltpu.PrefetchScalarGridSpec(
            num_scalar_prefetch=1, grid=(s, n_pages // ppc),
            in_specs=[pl.BlockSpec(memory_space=pltpu.SMEM), per_seq((rows, HA * 2 * DA)),
                      per_seq((8, HA * 2 * DA)), per_seq((8, HA * DVA)), _const_spec((1, DVA))]
                     + _page_specs(n_pages, ppc, HA * 2 * DA, page) + _page_specs(n_pages, ppc, HA * DVA, page),
            out_specs=per_seq((8, HA * DVA)),
            scratch_shapes=[pltpu.VMEM((rows, 1), F32), pltpu.VMEM((rows, HA * DVA + LANES), F32)]),
        compiler_params=_cparams(("parallel", "arbitrary")),
        name="diff_sample",
    )(pt, lam, qbd, k_new, v_new, hn, *([pool_k] * ppc), *([pool_v] * ppc))


def _cmp_sample_kernel(pt_ref, q_ref, kn_ref, vn_ref, w1k_ref, pek_ref, b1k_ref, w2k_ref, b2k_ref,
                       w1v_ref, pev_ref, b1v_ref, w2v_ref, b2v_ref, kng_ref, ones_ref, gsum_ref, *rest,
                       ppc, page, past_len, m_rows, nsb):
    k_refs, v_refs = rest[:ppc], rest[ppc:2 * ppc]
    o_ref, sel_ref, bufk, bufv = rest[2 * ppc:]
    c = pl.program_id(1)

    @pl.when(c == 0)
    def _():
        tail = bufk.shape[0] - past_len
        for buf, new in ((bufk, kn_ref), (bufv, vn_ref)):
            buf[pl.ds(past_len, tail), :] = jnp.zeros((tail, LANES), F32)
            buf[pl.ds(past_len, 8), :] = new[...]

    base = c * (ppc * page)
    for p in range(ppc):
        start = pl.multiple_of(base + p * page, page)
        bufk[pl.ds(start, page), :] = k_refs[p][...]
        bufv[pl.ds(start, page), :] = v_refs[p][...]

    @pl.when(c == pl.num_programs(1) - 1)
    def _():
        k = _compress(bufk, m_rows, w1k_ref, pek_ref, b1k_ref, w2k_ref, b2k_ref)
        kcmp_b = _group_norm(k, kng_ref[...], ones_ref[...]).astype(BF16)
        vcmp_b = _compress(bufv, m_rows, w1v_ref, pev_ref, b1v_ref, w2v_ref, b2v_ref).astype(BF16)
        qpos = past_len + lax.broadcasted_iota(jnp.int32, (8, 1), 0)
        gsum = gsum_ref[...]
        for kvh in range(KVH):
            q_ext = [q_ref[kvh * GB + g] for g in range(GB)]
            outs, sel = _cmp_attention(q_ext, kcmp_b, vcmp_b, qpos, SLOPES_B[kvh * GB:(kvh + 1) * GB], gsum, nsb, 8)
            for g in range(GB):
                o_ref[kvh * GB + g] = outs[g]
            sel_ref[kvh] = sel


def _cmp_sample(pt, q_heads, k_new, v_new, pool_k, pool_v, n_pages, past_len, n_new, wp):
    s = q_heads.shape[0]
    page = pool_k.shape[1]
    ppc = math.gcd(PAGES_PER_STEP, n_pages)
    tk_pad = _round_up(past_len + n_new, SEL_BLOCK)
    nc = tk_pad // CMP_STRIDE
    nsb = tk_pad // SEL_BLOCK
    m_rows = _round_up(nc, LANES)
    nsbp = _round_up(nsb, LANES)
    buf_rows = CMP_STRIDE * (m_rows - 1) + L_CMP
    gsum = ((jnp.arange(m_rows)[:, None] // (SEL_BLOCK // CMP_STRIDE) == jnp.arange(nsbp)[None, :])
            & (jnp.arange(m_rows)[:, None] < nc)).astype(BF16)
    cw = [_const_spec((L_CMP * LANES, KVH * CMP_HID)), _const_spec((1, L_CMP * LANES)), _const_spec((1, KVH * CMP_HID)),
          _const_spec((KVH * CMP_HID, LANES)), _const_spec((1, LANES))]
    per_seq = lambda shape: pl.BlockSpec((None,) + shape, lambda si, c, pt: (si,) + (0,) * len(shape))
    return pl.pallas_call(
        functools.partial(_cmp_sample_kernel, ppc=ppc, page=page, past_len=past_len, m_rows=m_rows, nsb=nsb),
        out_shape=(jax.ShapeDtypeStruct((s, HB, 8, LANES), F32), jax.ShapeDtypeStruct((s, KVH, 8, nsbp), F32)),
        grid_spec=pltpu.PrefetchScalarGridSpec(
            num_scalar_prefetch=1, grid=(s, n_pages // ppc),
            in_specs=[per_seq((HB, 8, LANES)), per_seq((8, LANES)), per_seq((8, LANES))] + cw + cw
                     + [_const_spec((1, LANES)), _const_spec((256, 256)), _const_spec((m_rows, nsbp))]
                     + _page_specs(n_pages, ppc, LANES, page) + _page_specs(n_pages, ppc, LANES, page),
            out_specs=(per_seq((HB, 8, LANES)), per_seq((KVH, 8, nsbp))),
            scratch_shapes=[pltpu.VMEM((buf_rows, LANES), F32), pltpu.VMEM((buf_rows, LANES), F32)]),
        compiler_params=_cparams(("parallel", "arbitrary")),
        name="cmp_sample",
    )(pt, q_heads, k_new, v_new, *wp["cmp_k"], *wp["cmp_v"], wp["g_kc"], wp["ones"], gsum,
      *([pool_k] * ppc), *([pool_v] * ppc))


def _nsa_sample_kernel(pt_ref, q_ref, sel_ref, e_ref, kn_ref, vn_ref, kwin_ref, vwin_ref, ocmp_ref, gate_ref, *rest,
                       ppc, page, past_len, n_new, nsb, w_buf):
    k_refs, v_refs = rest[:ppc], rest[ppc:2 * ppc]
    o_ref, m_ref, acc_ref = rest[2 * ppc:]
    c = pl.program_id(1)
    rows = HB * 8

    @pl.when(c == 0)
    def _():
        m_ref[...] = jnp.full(m_ref.shape, NEG_INF, F32)
        acc_ref[...] = jnp.zeros_like(acc_ref)

    ridx = lax.broadcasted_iota(jnp.int32, (rows, 1), 0)
    qpos = past_len + ridx % 8
    slope = jnp.zeros((rows, 1), F32)
    for h in range(HB):
        slope = jnp.where(ridx // 8 == h, SLOPES_B[h], slope)
    q = q_ref[...]
    selm = jnp.concatenate([sel_ref[kvh] for kvh in range(KVH) for _ in range(GB)], axis=0)

    def scores(kb, kpos):
        return _dot_nt(q, kb) - slope * (qpos - kpos).astype(F32)

    def update(s, valid, vb):
        v_ext = jnp.concatenate([vb, jnp.ones((vb.shape[0], LANES), BF16)], axis=1)
        m_new, acc = _flash_step(m_ref[...], acc_ref[...], s, valid, v_ext)
        m_ref[...] = m_new
        acc_ref[...] = acc

    chunk = ppc * page
    kb = jnp.concatenate([r[...] for r in k_refs], axis=0).astype(BF16)
    vb = jnp.concatenate([r[...] for r in v_refs], axis=0).astype(BF16)
    kpos = c * chunk + lax.broadcasted_iota(jnp.int32, (1, chunk), 1)
    mask = _dot(selm.astype(BF16), e_ref[...])
    update(scores(kb, kpos), mask > 0.5, vb)

    @pl.when(c == pl.num_programs(1) - 1)
    def _():
        kn = _pad_rows(kn_ref[...], LANES).astype(BF16)
        vn = _pad_rows(vn_ref[...], LANES).astype(BF16)
        npos = past_len + lax.broadcasted_iota(jnp.int32, (1, LANES), 1)
        blk_new = past_len // SEL_BLOCK
        lane_s = lax.broadcasted_iota(jnp.int32, selm.shape, 1)
        picked = jnp.sum(jnp.where(lane_s == blk_new, selm, 0.0), axis=-1, keepdims=True) > 0.5
        update(scores(kn, npos), picked & (npos <= qpos) & (npos < past_len + n_new), vn)
        acc = acc_ref[...]
        o_sel = acc[:, :LANES] / jnp.maximum(acc[:, LANES:], 1e-30)
        kw = kwin_ref[...].astype(BF16)
        vw = vwin_ref[...].astype(BF16)
        wpos = past_len - w_buf + lax.broadcasted_iota(jnp.int32, (1, kw.shape[0]), 1)
        dist = qpos - wpos
        valid = (dist >= 0) & (dist < WINDOW) & (wpos >= 0) & (wpos < past_len + n_new)
        s = jnp.where(valid, scores(kw, wpos), NEG_INF)
        mx = jnp.max(s, axis=-1, keepdims=True)
        e = jnp.where(valid, jnp.exp(s - mx), 0.0)
        p = e / jnp.maximum(jnp.sum(e, axis=-1, keepdims=True), 1e-30)
        o_win = _dot(p.astype(BF16), vw)
        g = gate_ref[...]
        o_ref[...] = g[:, 0:1] * ocmp_ref[...] + g[:, 1:2] * o_sel + g[:, 2:3] * o_win


def _nsa_sample(pt, qbd, sel, k_new, v_new, kwin, vwin, ocmp, gates, pool_k, pool_v, n_pages, past_len, n_new, w_buf):
    s = qbd.shape[0]
    page = pool_k.shape[1]
    ppc = math.gcd(PAGES_PER_STEP, n_pages)
    nsbp = sel.shape[-1]
    nsb = _round_up(past_len + n_new, SEL_BLOCK) // SEL_BLOCK
    rows = HB * 8
    emat = (jnp.arange(past_len)[None, :] // SEL_BLOCK == jnp.arange(nsbp)[:, None]).astype(BF16)
    per_seq = lambda shape: pl.BlockSpec((None,) + shape, lambda si, c, pt: (si,) + (0,) * len(shape))
    wrows = kwin.shape[1]
    return pl.pallas_call(
        functools.partial(_nsa_sample_kernel, ppc=ppc, page=page, past_len=past_len, n_new=n_new, nsb=nsb, w_buf=w_buf),
        out_shape=jax.ShapeDtypeStruct((s, rows, LANES), F32),
        grid_spec=pltpu.PrefetchScalarGridSpec(
            num_scalar_prefetch=1, grid=(s, n_pages // ppc),
            in_specs=[per_seq((rows, LANES)), per_seq((KVH, 8, nsbp)),
                      pl.BlockSpec((nsbp, ppc * page), lambda si, c, pt: (0, c)),
                      per_seq((8, LANES)), per_seq((8, LANES)), per_seq((wrows, LANES)), per_seq((wrows, LANES)),
                      per_seq((rows, LANES)), per_seq((rows, LANES))]
                     + _page_specs(n_pages, ppc, LANES, page) + _page_specs(n_pages, ppc, LANES, page),
            out_specs=per_seq((rows, LANES)),
            scratch_shapes=[pltpu.VMEM((rows, 1), F32), pltpu.VMEM((rows, 2 * LANES), F32)]),
        compiler_params=_cparams(("parallel", "arbitrary")),
        name="nsa_sample",
    )(pt, qbd, sel, emat, k_new, v_new, kwin, vwin, ocmp, gates, *([pool_k] * ppc), *([pool_v] * ppc))


def _prep_cmp(w1, pe, b1, w2, b2):
    eye = jnp.eye(KVH, dtype=F32)
    w1b = jnp.einsum('jde,hk->jhdke', w1, eye).reshape(L_CMP * KVH * DB, KVH * CMP_HID).astype(BF16)
    pef = jnp.tile(pe[:, None, :], (1, KVH, 1)).reshape(1, L_CMP * KVH * DB)
    w2b = jnp.einsum('ed,hk->hekd', w2, eye).reshape(KVH * CMP_HID, KVH * DB).astype(BF16)
    return (w1b, pef, jnp.tile(b1, KVH).reshape(1, -1), w2b, jnp.tile(b2, KVH).reshape(1, -1))


def _prep_layer(l, w_in, norm1_g, norm2_g, qn_a, kn_a, hn_a, qn_b, kn_cmp, kn_sel, kn_win, cmp_k, cmp_v,
                w_up_a, w_up_b, w_out, w_group, b_group, w_router, b_router, w_e_gate, w_e_up, w_e_down):
    d = w_in.shape[1]
    wi = w_in[l]
    o_qb = 2 * HA * 2 * DA + HA * DVA
    o_kv = o_qb + HB * DB
    o_gb = o_kv + 6 * KVH * DB
    o_gm = o_gb + 3 * HB
    wqb = wi[:, o_qb:o_kv].reshape(d, KVH, GB, DB)
    slot = jnp.zeros((d, KVH, GB, KVH, DB), F32)
    for kvh in range(KVH):
        slot = slot.at[:, kvh, :, kvh, :].set(wqb[:, kvh])
    wqb_ext = slot.reshape(d, HB * LANES)
    wgb = jnp.pad(wi[:, o_gb:o_gm], ((0, 0), (0, LANES - 3 * HB)))
    w_all = jnp.concatenate([wi[:, :o_qb], wqb_ext, wi[:, o_kv:o_gb], wgb, wi[:, o_gm:]], axis=1).astype(BF16)
    assert w_all.shape[1] == _W_COLS
    gq = jnp.zeros((KVH, GB, KVH, DB), F32)
    for kvh in range(KVH):
        gq = gq.at[kvh, :, kvh, :].set(jnp.broadcast_to(qn_b[l], (GB, DB)))
    ones = (jnp.arange(256)[:, None] // 64 == jnp.arange(256)[None, :] // 64).astype(BF16)
    w_r = jnp.concatenate([w_group[l], w_router[l].reshape(d, N_EXPERTS)], axis=1)
    w_r = jnp.pad(w_r, ((0, 0), (0, LANES - w_r.shape[1])))
    w_r_hi = w_r.astype(BF16)
    b_r = jnp.pad(jnp.concatenate([b_group[l], b_router[l].reshape(-1)]), (0, LANES - N_GROUPS - N_EXPERTS))
    return {
        "norm1_g": norm1_g[l].reshape(1, d), "norm2_g": norm2_g[l].reshape(1, d), "w_in": w_all,
        "g_qa": jnp.tile(qn_a[l], HA * 2).reshape(1, -1), "g_ka": jnp.tile(kn_a[l], HA * 2).reshape(1, -1),
        "g_qb": gq.reshape(1, -1), "g_ks": jnp.tile(kn_sel[l], KVH).reshape(1, -1),
        "g_kw": jnp.tile(kn_win[l], KVH).reshape(1, -1), "g_kc": jnp.tile(kn_cmp[l], KVH).reshape(1, -1),
        "hn_a": hn_a[l].reshape(1, DVA), "ones": ones,
        "cmp_k": _prep_cmp(*[a[l] for a in cmp_k]), "cmp_v": _prep_cmp(*[a[l] for a in cmp_v]),
        "w_up_a": w_up_a[l].astype(BF16), "w_up_b": w_up_b[l].astype(BF16), "w_out": w_out[l].astype(BF16),
        "w_r_hi": w_r_hi, "w_r_lo": (w_r - w_r_hi.astype(F32)).astype(BF16), "b_r": b_r.reshape(1, LANES),
        "w_gu": jnp.concatenate([w_e_gate[l], w_e_up[l]], axis=-1).astype(BF16), "w_down": w_e_down[l].astype(BF16),
    }


def kernel(x_prompt, x_sample, cache_diff_k, cache_diff_v, cache_cmp_k, cache_cmp_v, cache_sel_k, cache_sel_v, state_win_k, state_win_v, page_table, c_prompt, c_sample, w_ada, b_ada, norm1_g, norm2_g, w_in, qn_a, kn_a, hn_a, lam_q1, lam_k1, lam_q2, lam_k2, qn_b, kn_cmp, kn_sel, kn_win, cmp_pe_k, cmp_w1_k, cmp_b1_k, cmp_w2_k, cmp_b2_k, cmp_pe_v, cmp_w1_v, cmp_b1_v, cmp_w2_v, cmp_b2_v, w_up_a, w_up_b, w_out, w_group, b_group, w_router, b_router, w_e_gate, w_e_up, w_e_down):
    b, t, d = x_prompt.shape
    s, n_new, _ = x_sample.shape
    depth, n_phys, page = cache_diff_k.shape[:3]
    n_pages = page_table.shape[1]
    past_len = n_pages * page
    w_buf = state_win_k.shape[2]
    assert n_new <= 8 and past_len % SEL_BLOCK == 0 and t % LANES == 0

    pools = [p.reshape(depth * n_phys, page, -1) for p in
             (cache_diff_k, cache_diff_v, cache_cmp_k, cache_cmp_v, cache_sel_k, cache_sel_v)]
    slopes_a = jnp.asarray(SLOPES_A, F32)

    xp = x_prompt.reshape(b * t, d)
    xs = x_sample.reshape(s * n_new, d)
    c_all = jnp.concatenate([c_prompt, c_sample], axis=0)
    new_p = [[] for _ in range(8)]
    new_s = [[] for _ in range(8)]
    for l in range(depth):
        lambda_init = 0.8 - 0.6 * math.exp(-0.3 * l)
        out_scale = 1.0 - lambda_init
        wp = _prep_layer(l, w_in, norm1_g, norm2_g, qn_a, kn_a, hn_a, qn_b, kn_cmp, kn_sel, kn_win,
                         (cmp_w1_k, cmp_pe_k, cmp_b1_k, cmp_w2_k, cmp_b2_k),
                         (cmp_w1_v, cmp_pe_v, cmp_b1_v, cmp_w2_v, cmp_b2_v),
                         w_up_a, w_up_b, w_out, w_group, b_group, w_router, b_router, w_e_gate, w_e_up, w_e_down)
        mod, lam_row = _ada(c_all, w_ada[l], b_ada[l], lam_q1[l], lam_k1[l], lam_q2[l], lam_k2[l], lambda_init)
        lam = lam_row[0, :1]
        mods_p = [m.reshape(b, 1, d) for m in jnp.split(mod[:b], 6, axis=-1)]
        mods_s = [jnp.repeat(m, n_new, axis=0) for m in jnp.split(mod[b:], 6, axis=-1)]

        (qa, ka, va, qb, kc, vc, ks, vs, kw, vw, gbs, gm) = _proj(xp, mods_p[1], mods_p[0], False, t, wp)
        r3 = lambda a: a.reshape(b, t, -1)
        oa = _diff_prompt(r3(qa), r3(ka), r3(va), slopes_a, lam, wp["hn_a"], out_scale)
        ocmp, sel = _cmp_prompt(r3(qb), r3(kc), r3(vc), wp)
        ob = _nsa_prompt(r3(qb), r3(ks), r3(vs), r3(kw), r3(vw), sel, ocmp, r3(gbs))
        x1, h2, cw = _merge(oa.reshape(b * t, -1), ob.reshape(b * t, -1), gm, xp, mods_p[2], mods_p[4], mods_p[3],
                            False, t, wp)
        xp = _moe(h2, cw, x1, mods_p[5], False, t, wp)
        wp_rows = min(WINDOW, t)
        rows_p = [ka.reshape(b, t, HA, 2, DA), va.reshape(b, t, HA, DVA)] + \
                 [a.reshape(b, t, KVH, DB) for a in (kc, vc, ks, vs)] + \
                 [a.reshape(b, t, KVH, DB)[:, -wp_rows:] for a in (kw, vw)]

        (qa, ka, va, qb, kc, vc, ks, vs, kw, vw, gbs, gm) = _proj(xs, mods_s[1], mods_s[0], True, n_new, wp)
        pt = (page_table + l * n_phys).reshape(-1).astype(jnp.int32)
        pad8 = lambda a: jnp.pad(a.reshape(s, n_new, -1), ((0, 0), (0, 8 - n_new), (0, 0)))
        qa8 = pad8(qa)
        own = (jnp.arange(HA * 2 * DA)[None, :] // DA == jnp.arange(HA * 2)[:, None]).astype(BF16)
        qbd = (qa8[:, None, :, :] * own[None, :, None, :]).reshape(s, HA * 2 * 8, HA * 2 * DA)
        oa_s = _diff_sample(pt, lam, qbd, pad8(ka), pad8(va), wp["hn_a"], pools[0], pools[1],
                            n_pages, past_len, n_new, out_scale)
        oa_s = oa_s[:, :n_new].reshape(s * n_new, HA * DVA)
        q_heads = pad8(qb).reshape(s, 8, HB, LANES).transpose(0, 2, 1, 3)
        ocmp_s, sel_s = _cmp_sample(pt, q_heads, pad8(kc), pad8(vc), pools[2], pools[3], n_pages, past_len, n_new, wp)
        wrows = _round_up(w_buf + n_new, LANES)
        kwin = jnp.concatenate([state_win_k[l].reshape(s, w_buf, -1), kw.reshape(s, n_new, -1)], axis=1)
        vwin = jnp.concatenate([state_win_v[l].reshape(s, w_buf, -1), vw.reshape(s, n_new, -1)], axis=1)
        padw = lambda a: jnp.pad(a, ((0, 0), (0, wrows - a.shape[1]), (0, 0)))
        gates = pad8(gbs)[:, :, :3 * HB].reshape(s, 8, HB, 3).transpose(0, 2, 1, 3).reshape(s, HB * 8, 3)
        gates = jnp.pad(gates, ((0, 0), (0, 0), (0, LANES - 3)))
        ob_s = _nsa_sample(pt, q_heads.reshape(s, HB * 8, LANES), sel_s, pad8(ks), pad8(vs), padw(kwin), padw(vwin),
                           ocmp_s.reshape(s, HB * 8, LANES), gates, pools[4], pools[5], n_pages, past_len, n_new, w_buf)
        ob_s = ob_s.reshape(s, KVH, GB, 8, KVH, DB)
        ob_s = jnp.stack([ob_s[:, kvh, :, :, kvh, :] for kvh in range(KVH)], axis=1)
        ob_s = ob_s.transpose(0, 3, 1, 2, 4)[:, :n_new].reshape(s * n_new, HB * DB).astype(BF16)
        x1, h2, cw = _merge(oa_s, ob_s, gm, xs, mods_s[2], mods_s[4], mods_s[3], True, n_new, wp)
        xs = _moe(h2, cw, x1, mods_s[5], True, n_new, wp)
        rows_s = [ka.reshape(s, n_new, HA, 2, DA), va.reshape(s, n_new, HA, DVA)] + \
                 [a.reshape(s, n_new, KVH, DB) for a in (kc, vc, ks, vs)] + \
                 [kwin[:, -w_buf:].reshape(s, w_buf, KVH, DB), vwin[:, -w_buf:].reshape(s, w_buf, KVH, DB)]
        for i in range(8):
            new_p[i].append(rows_p[i])
            new_s[i].append(rows_s[i])

    outs_p = [jnp.stack(a) for a in new_p]
    outs_s = [jnp.stack(a) for a in new_s]
    return (xp.reshape(b, t, d), xs.reshape(s, n_new, d), *outs_p, *outs_s)
```

```python
import functools
import math

import jax
import jax.numpy as jnp
from jax import lax
from jax.experimental import pallas as pl
from jax.experimental.pallas import tpu as pltpu

F32 = jnp.float32
BF16 = jnp.bfloat16

HA, DA, DVA = 4, 64, 128
HB, KVH, GB, DB = 8, 2, 4, 64
L_CMP, CMP_STRIDE, CMP_HID = 32, 16, 128
SEL_BLOCK, N_SEL, WINDOW = 64, 8, 512
N_GROUPS, EXPERTS_PER_GROUP, N_EXPERTS, D_EXPERT = 4, 4, 16, 256
RMS_EPS = 1e-6
NEG_INF = -1e30
BIG = 1e9
REMOVED = -3e38
MASK_NEG = 1e30
AUG_POS = DB - 2
SLOPES_A = tuple(2.0 ** (-8.0 * (i + 1) / HA) for i in range(HA))
SLOPES_B = tuple(2.0 ** (-8.0 * (i + 1) / HB) for i in range(HB))
QK_SCALE = 0.125

LANES = 128
VMEM_LIMIT = 56 * 1024 * 1024

ROW_TILE = 256
MOE_ROW_TILE = 512
DIFF_TQ = 256
NSA_TQ = 128
NSA_TK = 256
CMP_TQ = 256
PAGES_PER_STEP = 8


def _cparams(sem):
    return pltpu.CompilerParams(dimension_semantics=sem, vmem_limit_bytes=VMEM_LIMIT)


def _const_spec(shape):
    nd = len(shape)
    return pl.BlockSpec(shape, lambda *_: (0,) * nd)


def _round_up(x, m):
    return -(-x // m) * m


def _split3(x):
    x1 = x.astype(BF16)
    r = x - x1.astype(F32)
    x2 = r.astype(BF16)
    x3 = (r - x2.astype(F32)).astype(BF16)
    return x1, x2, x3


def _dot(a, b):
    return jnp.dot(a, b, preferred_element_type=F32)


def _dot_nt(a, b):
    return lax.dot_general(a, b, (((1,), (1,)), ((), ())), preferred_element_type=F32)


def _gelu_tanh(x):
    return 0.5 * x * (1.0 + jnp.tanh(0.7978845608028654 * (x + 0.044715 * x * x * x)))


def _topk_mask(imp, k):
    width = imp.shape[-1]
    lane = lax.broadcasted_iota(jnp.int32, imp.shape, 1).astype(F32)
    sel = jnp.zeros(imp.shape, F32)
    work = imp
    for _ in range(k):
        m = jnp.max(work, axis=-1, keepdims=True)
        idx = jnp.min(jnp.where(work == m, lane, float(width)), axis=-1, keepdims=True)
        hit = lane == idx
        sel = jnp.where(hit, 1.0, sel)
        work = jnp.where(hit, REMOVED, work)
    return sel


def _ada_kernel(c_ref, w_ref, b_ref, q1_ref, k1_ref, q2_ref, k2_ref, o_ref, lam_ref, *, lambda_init):
    c = c_ref[...]
    s = c * jax.nn.sigmoid(c)
    s1, s2, s3 = _split3(s)
    w1, w2, w3 = _split3(w_ref[...])
    acc = _dot(s1, w1) + (_dot(s1, w2) + _dot(s2, w1)) + (_dot(s1, w3) + _dot(s2, w2) + _dot(s3, w1))
    o_ref[...] = acc + b_ref[...]
    a = jnp.sum(q1_ref[...] * k1_ref[...], axis=-1, keepdims=True)
    b = jnp.sum(q2_ref[...] * k2_ref[...], axis=-1, keepdims=True)
    lam_ref[...] = jnp.broadcast_to(jnp.exp(a) - jnp.exp(b) + lambda_init, lam_ref.shape)


def _ada(c_all, w_ada, b_ada, lq1, lk1, lq2, lk2, lambda_init):
    n, d = c_all.shape
    n_out = w_ada.shape[1]
    tn = 512
    vec = pl.BlockSpec((1, DA), lambda j: (0, 0))
    return pl.pallas_call(
        functools.partial(_ada_kernel, lambda_init=lambda_init),
        out_shape=(jax.ShapeDtypeStruct((n, n_out), F32), jax.ShapeDtypeStruct((1, LANES), F32)),
        grid=(n_out // tn,),
        in_specs=[pl.BlockSpec((n, d), lambda j: (0, 0)), pl.BlockSpec((d, tn), lambda j: (0, j)),
                  pl.BlockSpec((1, tn), lambda j: (0, j)), vec, vec, vec, vec],
        out_specs=(pl.BlockSpec((n, tn), lambda j: (0, j)), pl.BlockSpec((1, LANES), lambda j: (0, 0))),
        compiler_params=_cparams(("arbitrary",)),
        name="ada",
    )(c_all, w_ada, b_ada.reshape(1, -1), lq1.reshape(1, -1), lk1.reshape(1, -1), lq2.reshape(1, -1), lk2.reshape(1, -1))


_W_QA, _W_KA, _W_VA, _W_QB = 0, 512, 1024, 1536
_W_KV6 = _W_QB + HB * LANES
_W_GB = _W_KV6 + 6 * 128
_W_GM = _W_GB + 128
_W_COLS = _W_GM + 2048


def _group_norm(p, gain, ones):
    n = p.shape[1]
    w = 256 if n % 256 == 0 else 128
    sq = (p * p).astype(BF16)
    parts = [_dot(sq[:, c:c + w], ones[:w, :w]) for c in range(0, n, w)]
    ss = parts[0] if len(parts) == 1 else jnp.concatenate(parts, axis=1)
    return p * lax.rsqrt(ss * (1.0 / 64.0) + RMS_EPS) * gain


def _proj_kernel(x_ref, sc_ref, sh_ref, ng_ref, w_ref, gqa_ref, gka_ref, gqb_ref, gks_ref, gkw_ref, ones_ref,
                 qa_ref, ka_ref, va_ref, qb_ref, kc_ref, vc_ref, ks_ref, vs_ref, kw_ref, vw_ref, gb_ref, gm_ref):
    x = x_ref[...]
    h = x * lax.rsqrt(jnp.mean(x * x, axis=-1, keepdims=True) + RMS_EPS) * ng_ref[...]
    h = h * (1.0 + sc_ref[...]) + sh_ref[...]
    hb = h.astype(BF16)
    ones = ones_ref[...]

    def seg(a, n):
        return _dot(hb, w_ref[:, a:a + n])

    qa_ref[...] = (_group_norm(seg(_W_QA, 512), gqa_ref[...], ones) * QK_SCALE).astype(BF16)
    ka_ref[...] = _group_norm(seg(_W_KA, 512), gka_ref[...], ones)
    va_ref[...] = seg(_W_VA, 512)
    qb_ref[...] = (_group_norm(seg(_W_QB, 1024), gqb_ref[...], ones) * QK_SCALE).astype(BF16)
    kc_ref[...] = seg(_W_KV6, 128)
    vc_ref[...] = seg(_W_KV6 + 128, 128)
    ks_ref[...] = _group_norm(seg(_W_KV6 + 256, 128), gks_ref[...], ones)
    vs_ref[...] = seg(_W_KV6 + 384, 128)
    kw_ref[...] = _group_norm(seg(_W_KV6 + 512, 128), gkw_ref[...], ones)
    vw_ref[...] = seg(_W_KV6 + 640, 128)
    gb_ref[...] = jax.nn.sigmoid(seg(_W_GB, 128))
    gm_ref[:, :1024] = jax.nn.sigmoid(seg(_W_GM, 1024)).astype(BF16)
    gm_ref[:, 1024:] = jax.nn.sigmoid(seg(_W_GM + 1024, 1024)).astype(BF16)


def _mod_spec(per_token, tm, bps, d):
    if per_token:
        return pl.BlockSpec((tm, d), lambda i, *_: (i, 0))
    return pl.BlockSpec((None, 1, d), lambda i, *_: (i // bps, 0, 0))


def _proj(x, sc, sh, per_token, rows_per_seq, wp):
    n, d = x.shape
    tm = min(ROW_TILE, n)
    bps = max(rows_per_seq // tm, 1)
    mod = _mod_spec(per_token, tm, bps, d)

    def row(width):
        return pl.BlockSpec((tm, width), lambda i: (i, 0))

    widths = [(512, BF16), (512, F32), (512, F32), (1024, BF16)] + [(128, F32)] * 6 + [(128, F32), (2048, BF16)]
    return pl.pallas_call(
        _proj_kernel,
        out_shape=tuple(jax.ShapeDtypeStruct((n, w), dt) for w, dt in widths),
        grid=(n // tm,),
        in_specs=[row(d), mod, mod, _const_spec((1, d)), _const_spec((d, _W_COLS)),
                  _const_spec((1, 512)), _const_spec((1, 512)), _const_spec((1, 1024)),
                  _const_spec((1, 128)), _const_spec((1, 128)), _const_spec((256, 256))],
        out_specs=tuple(row(w) for w, _ in widths),
        compiler_params=_cparams(("parallel",)),
        name="proj",
    )(x, sc, sh, wp["norm1_g"], wp["w_in"], wp["g_qa"], wp["g_ka"], wp["g_qb"], wp["g_ks"], wp["g_kw"], wp["ones"])


POS_RADIX = 256


def _pos_lanes(n_keys):
    k = jnp.arange(n_keys)
    return jnp.stack([k // POS_RADIX, k % POS_RADIX], axis=1).astype(F32)


def _flash_step(m, acc, s, valid, v_ext):
    if valid is not None:
        s = jnp.where(valid, s, NEG_INF)
    m_new = jnp.maximum(m, jnp.max(s, axis=-1, keepdims=True))
    alpha = jnp.exp(m - m_new)
    p = jnp.exp(s - m_new)
    acc = alpha * acc + _dot(p.astype(BF16), v_ext)
    return m_new, acc


DIFF_HEADS_PER_STEP = 2


def _diff_prompt_kernel(slope_ref, lam_ref, q_ref, k_ref, v_ref, pos_ref, hn_ref, o_ref, ka0, ka1, vext,
                        *, tq, out_scale):
    hp = pl.program_id(1)
    i = pl.program_id(2)
    lam = lam_ref[0]
    nh = DIFF_HEADS_PER_STEP

    @pl.when(i == 0)
    def _():
        pos = pos_ref[...]
        lane = lax.broadcasted_iota(jnp.int32, pos.shape, 1)
        for hh in range(nh):
            k = k_ref[:, hh * 2 * DA:(hh + 1) * 2 * DA]
            ka0[hh] = jnp.where(lane < DA, k, pos).astype(BF16)
            ka1[hh] = jnp.where(lane >= DA, k, pos).astype(BF16)
            vext[hh, :, :DVA] = v_ref[:, hh * DVA:(hh + 1) * DVA].astype(BF16)
            vext[hh, :, DVA:] = jnp.ones((vext.shape[1], DVA), BF16)

    lane = lax.broadcasted_iota(jnp.int32, (tq, 2 * DA), 1)
    half = lane % DA
    chains = []
    for hh in range(nh):
        slope = slope_ref[hp * nh + hh]
        q = q_ref[:, hh * 2 * DA:(hh + 1) * 2 * DA].astype(F32)
        digits = jnp.where(half == 0, slope * POS_RADIX, jnp.where(half == 1, slope, 0.0))
        chains.append((hh, ka0, jnp.where(lane < DA, q, digits).astype(BF16)))
        chains.append((hh, ka1, jnp.where(lane >= DA, q, digits).astype(BF16)))
    r = lax.broadcasted_iota(jnp.int32, (tq, tq), 0)
    c = lax.broadcasted_iota(jnp.int32, (tq, tq), 1)
    causal = r >= c

    def block(j, carry, valid):
        start = pl.multiple_of(j * tq, tq)
        out = []
        for (hh, kaug, qm), (m, acc) in zip(chains, carry):
            s = _dot_nt(qm, kaug[hh, pl.ds(start, tq), :])
            out.append(_flash_step(m, acc, s, valid, vext[hh, pl.ds(start, tq), :]))
        return tuple(out)

    init = tuple((jnp.full((tq, 1), NEG_INF, F32), jnp.zeros((tq, 2 * DVA), F32)) for _ in chains)
    carry = lax.fori_loop(0, i, lambda j, cr: block(j, cr, None), init)
    carry = block(i, carry, causal)
    outs = [acc[:, :DVA] / acc[:, DVA:] for _, acc in carry]
    for hh in range(nh):
        o = outs[2 * hh] - lam * outs[2 * hh + 1]
        o = o * lax.rsqrt(jnp.mean(o * o, axis=-1, keepdims=True) + RMS_EPS) * hn_ref[...]
        o_ref[:, hh * DVA:(hh + 1) * DVA] = (o * out_scale).astype(BF16)


def _diff_prompt(qa, ka, va, slopes, lam, hn, out_scale):
    b, t, _ = qa.shape
    tq = min(DIFF_TQ, t)
    nh = DIFF_HEADS_PER_STEP
    assert t <= POS_RADIX * POS_RADIX and HA % nh == 0
    smem = pl.BlockSpec(memory_space=pltpu.SMEM)
    pos = jnp.pad(_pos_lanes(t), ((0, 0), (0, DA - 2)))
    pos = jnp.concatenate([pos, pos], axis=1)
    return pl.pallas_call(
        functools.partial(_diff_prompt_kernel, tq=tq, out_scale=out_scale),
        out_shape=jax.ShapeDtypeStruct((b, t, HA * DVA), BF16),
        grid=(b, HA // nh, t // tq),
        in_specs=[smem, smem,
                  pl.BlockSpec((None, tq, nh * 2 * DA), lambda bi, h, i: (bi, i, h)),
                  pl.BlockSpec((None, t, nh * 2 * DA), lambda bi, h, i: (bi, 0, h)),
                  pl.BlockSpec((None, t, nh * DVA), lambda bi, h, i: (bi, 0, h)),
                  pl.BlockSpec((t, 2 * DA), lambda bi, h, i: (0, 0)),
                  pl.BlockSpec((1, DVA), lambda bi, h, i: (0, 0))],
        out_specs=pl.BlockSpec((None, tq, nh * DVA), lambda bi, h, i: (bi, i, h)),
        scratch_shapes=[pltpu.VMEM((nh, t, 2 * DA), BF16), pltpu.VMEM((nh, t, 2 * DA), BF16),
                        pltpu.VMEM((nh, t, 2 * DVA), BF16)],
        compiler_params=_cparams(("parallel", "parallel", "arbitrary")),
        name="diff_prompt",
    )(slopes, lam, qa, ka, va, pos, hn)


def _compress(buf_ref, m_rows, w1_ref, pe_ref, b1_ref, w2_ref, b2_ref):
    acc = jnp.zeros((m_rows, KVH * CMP_HID), F32)
    group = 8
    for jg in range(L_CMP // group):
        cols = slice(jg * group * LANES, (jg + 1) * group * LANES)
        xs = [buf_ref[pl.ds(jg * group + jj, m_rows, stride=CMP_STRIDE), :] for jj in range(group)]
        x = jnp.concatenate(xs, axis=1) + pe_ref[:, cols]
        acc = acc + _dot(x.astype(BF16), w1_ref[cols, :])
    hid = _gelu_tanh(acc + b1_ref[...])
    return _dot(hid.astype(BF16), w2_ref[...]) + b2_ref[...]


def _cmp_attention(q_ext, kcmp_b, vcmp_b, qpos, slopes_h, gsum, nsb, n_q_rows):
    m_rows = kcmp_b.shape[0]
    cend = (lax.broadcasted_iota(jnp.int32, (1, m_rows), 1) * CMP_STRIDE + (L_CMP - 1))
    visible = cend <= qpos
    dist = (qpos - cend).astype(F32)
    outs = []
    psum = jnp.zeros((n_q_rows, m_rows), F32)
    for g in range(GB):
        s = _dot_nt(q_ext[g], kcmp_b) - slopes_h[g] * dist
        s = jnp.where(visible, s, NEG_INF)
        mx = jnp.max(s, axis=-1, keepdims=True)
        e = jnp.where(visible, jnp.exp(s - mx), 0.0)
        p = e / jnp.maximum(jnp.sum(e, axis=-1, keepdims=True), 1e-30)
        outs.append(_dot(p.astype(BF16), vcmp_b))
        psum = psum + p
    return outs, _select_blocks(psum, qpos, gsum, nsb)


def _select_blocks(psum, qpos, gsum, nsb):
    p1, p2, p3 = _split3(psum)
    imp = _dot(p1, gsum) + _dot(p2, gsum) + _dot(p3, gsum)
    blk = lax.broadcasted_iota(jnp.int32, (1, imp.shape[1]), 1)
    cur = qpos // SEL_BLOCK
    forced = (blk == 0) | (blk == cur) | (blk == cur - 1)
    imp = jnp.where(blk > cur, -BIG, imp)
    imp = jnp.where(forced, BIG, imp)
    imp = jnp.where(blk >= nsb, REMOVED, imp)
    return _topk_mask(imp, min(N_SEL, nsb))


def _cmp_prompt_kernel(q_ref, kc_ref, vc_ref, w1k_ref, pek_ref, b1k_ref, w2k_ref, b2k_ref,
                       w1v_ref, pev_ref, b1v_ref, w2v_ref, b2v_ref, kn_ref, ones_ref, gsum_ref,
                       o_ref, sel_ref, bufk, bufv, kcmp, vcmp, *, t, tq, m_rows, nsb):
    i = pl.program_id(1)

    @pl.when(i == 0)
    def _():
        for buf, src in ((bufk, kc_ref), (bufv, vc_ref)):
            buf[pl.ds(0, t), :] = src[...]
            buf[pl.ds(t, buf.shape[0] - t), :] = jnp.zeros((buf.shape[0] - t, LANES), F32)
        k = _compress(bufk, m_rows, w1k_ref, pek_ref, b1k_ref, w2k_ref, b2k_ref)
        kcmp[...] = _group_norm(k, kn_ref[...], ones_ref[...]).astype(BF16)
        vcmp[...] = _compress(bufv, m_rows, w1v_ref, pev_ref, b1v_ref, w2v_ref, b2v_ref).astype(BF16)

    qpos = i * tq + lax.broadcasted_iota(jnp.int32, (tq, 1), 0)
    kcmp_b = kcmp[...]
    vcmp_b = vcmp[...]
    gsum = gsum_ref[...]
    for kvh in range(KVH):
        q_ext = [q_ref[:, (kvh * GB + g) * LANES:(kvh * GB + g + 1) * LANES] for g in range(GB)]
        outs, sel = _cmp_attention(q_ext, kcmp_b, vcmp_b, qpos, SLOPES_B[kvh * GB:(kvh + 1) * GB], gsum, nsb, tq)
        for g in range(GB):
            o_ref[:, (kvh * GB + g) * LANES:(kvh * GB + g + 1) * LANES] = outs[g].astype(BF16)
        blk = lax.broadcasted_iota(jnp.int32, sel.shape, 1)
        neg = jnp.where(blk < nsb, (sel - 1.0) * MASK_NEG, 0.0)
        if kvh == 0:
            neg = pltpu.roll(neg, DB, 1)
        sel_ref[kvh] = neg.astype(BF16)


def _cmp_prompt(qb_ext, kc, vc, wp):
    b, t, _ = qb_ext.shape
    tq = min(CMP_TQ, t)
    nc = t // CMP_STRIDE
    m_rows = _round_up(nc, LANES)
    nsb = t // SEL_BLOCK
    nsbp = _round_up(nsb, LANES)
    assert nsbp == LANES and nsb <= AUG_POS
    buf_rows = CMP_STRIDE * (m_rows - 1) + L_CMP
    gsum = ((jnp.arange(m_rows)[:, None] // (SEL_BLOCK // CMP_STRIDE) == jnp.arange(nsbp)[None, :])
            & (jnp.arange(m_rows)[:, None] < nc)).astype(BF16)
    kv = pl.BlockSpec((None, t, LANES), lambda bi, i: (bi, 0, 0))
    cw = [_const_spec((L_CMP * LANES, KVH * CMP_HID)), _const_spec((1, L_CMP * LANES)), _const_spec((1, KVH * CMP_HID)),
          _const_spec((KVH * CMP_HID, LANES)), _const_spec((1, LANES))]
    return pl.pallas_call(
        functools.partial(_cmp_prompt_kernel, t=t, tq=tq, m_rows=m_rows, nsb=nsb),
        out_shape=(jax.ShapeDtypeStruct((b, t, HB * LANES), BF16), jax.ShapeDtypeStruct((b, KVH, t, nsbp), BF16)),
        grid=(b, t // tq),
        in_specs=[pl.BlockSpec((None, tq, HB * LANES), lambda bi, i: (bi, i, 0)), kv, kv] + cw + cw
                 + [_const_spec((1, LANES)), _const_spec((256, 256)), _const_spec((m_rows, nsbp))],
        out_specs=(pl.BlockSpec((None, tq, HB * LANES), lambda bi, i: (bi, i, 0)),
                   pl.BlockSpec((None, KVH, tq, nsbp), lambda bi, i: (bi, 0, i, 0))),
        scratch_shapes=[pltpu.VMEM((buf_rows, LANES), F32), pltpu.VMEM((buf_rows, LANES), F32),
                        pltpu.VMEM((m_rows, LANES), BF16), pltpu.VMEM((m_rows, LANES), BF16)],
        compiler_params=_cparams(("parallel", "arbitrary")),
        name="cmp_prompt",
    )(qb_ext, kc, vc, *wp["cmp_k"], *wp["cmp_v"], wp["g_kc"], wp["ones"], gsum)


def _compact_pairs(ext, kvh):
    lane = lax.broadcasted_iota(jnp.int32, ext[0].shape, 1)
    pieces = []
    for pair in range(GB // 2):
        even, odd = ext[2 * pair], ext[2 * pair + 1]
        if kvh == 0:
            pieces.append(jnp.where(lane < DB, even, pltpu.roll(odd, DB, 1)))
        else:
            pieces.append(jnp.where(lane < DB, pltpu.roll(even, DB, 1), odd))
    return jnp.concatenate(pieces, axis=1)


def _nsa_prompt_kernel(q_ref, ks_ref, vs_ref, kw_ref, vw_ref, sel_ref, ocmp_ref, gb_ref, aug_ref, o_ref,
                       ka_s, va_s, ka_w, va_w, *, tq, tk):
    i = pl.program_id(1)
    rows = 2 * tq

    @pl.when(i == 0)
    def _():
        lane = lax.broadcasted_iota(jnp.int32, ks_ref.shape, 1)
        for kvh in range(KVH):
            own = (lane >= kvh * DB) & (lane < (kvh + 1) * DB)
            aug = aug_ref[kvh]
            ka_s[kvh] = jnp.where(own, ks_ref[...], aug).astype(BF16)
            ka_w[kvh] = jnp.where(own, kw_ref[...], aug).astype(BF16)
            va_s[kvh] = jnp.where(own, vs_ref[...], 1.0).astype(BF16)
            va_w[kvh] = jnp.where(own, vw_ref[...], 1.0).astype(BF16)

    r = lax.broadcasted_iota(jnp.int32, (rows, tk), 0) % tq
    c = lax.broadcasted_iota(jnp.int32, (rows, tk), 1)
    dbase = r - c
    qlane = lax.broadcasted_iota(jnp.int32, (tq, LANES), 1)
    gates = gb_ref[...]
    j_hi = (i * tq + tq - 1) // tk
    j_lo_win = jnp.maximum(i * tq - (WINDOW - 1), 0) // tk

    chains = []
    for kvh in range(KVH):
        base = (1 - kvh) * DB
        own_q = (qlane >= kvh * DB) & (qlane < (kvh + 1) * DB)
        negmask = sel_ref[kvh].astype(F32)
        q_sel, q_win = [], []
        for g in range(GB):
            head = kvh * GB + g
            slot = q_ref[:, head * LANES:(head + 1) * LANES].astype(F32)
            digits = jnp.where(qlane == base + AUG_POS, SLOPES_B[head] * POS_RADIX,
                               jnp.where(qlane == base + AUG_POS + 1, SLOPES_B[head], 0.0))
            q_win.append(jnp.where(own_q, slot, digits).astype(BF16))
            q_sel.append(jnp.where(own_q, slot, digits + negmask).astype(BF16))
        for g in range(0, GB, 2):
            chains.append((kvh, g, jnp.concatenate(q_sel[g:g + 2], axis=0), jnp.concatenate(q_win[g:g + 2], axis=0)))

    def sel_block(j, carry, valid=None):
        start = pl.multiple_of(j * tk, tk)
        out = []
        for (kvh, _, qs, _), (m, acc) in zip(chains, carry):
            s = _dot_nt(qs, ka_s[kvh, pl.ds(start, tk), :])
            out.append(_flash_step(m, acc, s, valid, va_s[kvh, pl.ds(start, tk), :]))
        return tuple(out)

    def win_block(j, carry):
        start = pl.multiple_of(j * tk, tk)
        dist = dbase - (j * tk - i * tq)
        valid = (dist >= 0) & (dist < WINDOW)
        out = []
        for (kvh, _, _, qw), (m, acc) in zip(chains, carry):
            s = _dot_nt(qw, ka_w[kvh, pl.ds(start, tk), :])
            out.append(_flash_step(m, acc, s, valid, va_w[kvh, pl.ds(start, tk), :]))
        return tuple(out)

    init = tuple((jnp.full((rows, 1), NEG_INF, F32), jnp.zeros((rows, LANES), F32)) for _ in chains)
    carry = lax.fori_loop(0, j_hi, sel_block, init)
    res_s = sel_block(j_hi, carry, dbase >= j_hi * tk - i * tq)
    res_w = lax.fori_loop(j_lo_win, j_hi + 1, win_block, init)

    def finish(acc):
        den = pltpu.roll(acc, DB, 1)
        return acc / jnp.maximum(den, 1e-30)

    ext = [[None] * GB for _ in range(KVH)]
    for (kvh, g0, _, _), (_, acc_s), (_, acc_w) in zip(chains, res_s, res_w):
        o_s, o_w = finish(acc_s), finish(acc_w)
        for gi in range(2):
            g = g0 + gi
            head = kvh * GB + g
            col = head * 3
            oc = ocmp_ref[:, head * LANES:(head + 1) * LANES].astype(F32)
            ext[kvh][g] = (gates[:, col:col + 1] * oc + gates[:, col + 1:col + 2] * o_s[gi * tq:(gi + 1) * tq]
                           + gates[:, col + 2:col + 3] * o_w[gi * tq:(gi + 1) * tq])
    for kvh in range(KVH):
        o_ref[:, kvh * GB * DB:(kvh + 1) * GB * DB] = _compact_pairs(ext[kvh], kvh).astype(BF16)


def _nsa_prompt(qb_ext, ks, vs, kw, vw, sel, ocmp, gbs):
    b, t, _ = qb_ext.shape
    tq = min(NSA_TQ, t)
    tk = min(NSA_TK, t)
    nsb = t // SEL_BLOCK
    assert tk % tq == 0 and nsb <= AUG_POS and t <= POS_RADIX * POS_RADIX
    half = jnp.concatenate([(jnp.arange(t)[:, None] // SEL_BLOCK == jnp.arange(AUG_POS)[None, :]).astype(F32),
                            _pos_lanes(t)], axis=1)
    zeros = jnp.zeros((t, DB), F32)
    aug = jnp.stack([jnp.concatenate([zeros, half], axis=1), jnp.concatenate([half, zeros], axis=1)])
    kv = pl.BlockSpec((None, t, LANES), lambda bi, i: (bi, 0, 0))
    scratch = pltpu.VMEM((KVH, t, LANES), BF16)
    return pl.pallas_call(
        functools.partial(_nsa_prompt_kernel, tq=tq, tk=tk),
        out_shape=jax.ShapeDtypeStruct((b, t, HB * DB), BF16),
        grid=(b, t // tq),
        in_specs=[pl.BlockSpec((None, tq, HB * LANES), lambda bi, i: (bi, i, 0)), kv, kv, kv, kv,
                  pl.BlockSpec((None, KVH, tq, LANES), lambda bi, i: (bi, 0, i, 0)),
                  pl.BlockSpec((None, tq, HB * LANES), lambda bi, i: (bi, i, 0)),
                  pl.BlockSpec((None, tq, LANES), lambda bi, i: (bi, i, 0)),
                  _const_spec((KVH, t, LANES))],
        out_specs=pl.BlockSpec((None, tq, HB * DB), lambda bi, i: (bi, i, 0)),
        scratch_shapes=[scratch, scratch, scratch, scratch],
        compiler_params=_cparams(("parallel", "arbitrary")),
        name="nsa_prompt",
    )(qb_ext, ks, vs, kw, vw, sel, ocmp, gbs, aug)


def _merge_kernel(oa_ref, ob_ref, gm_ref, x_ref, g1_ref, sc_ref, sh_ref, ng_ref, wa_ref, wb_ref, wo_ref,
                  wrh_ref, wrl_ref, br_ref, x1_ref, h2_ref, cw_ref):
    ya = _dot(oa_ref[...], wa_ref[...])
    yb = _dot(ob_ref[...], wb_ref[...])
    d = ya.shape[1]
    t = gm_ref[:, :d].astype(F32) * ya + gm_ref[:, d:].astype(F32) * yb
    mix = _dot(t.astype(BF16), wo_ref[...])
    x1 = x_ref[...] + g1_ref[...] * mix
    x1_ref[...] = x1
    h2 = x1 * lax.rsqrt(jnp.mean(x1 * x1, axis=-1, keepdims=True) + RMS_EPS) * ng_ref[...]
    h2 = h2 * (1.0 + sc_ref[...]) + sh_ref[...]
    h2_ref[...] = h2.astype(BF16)
    hh = h2.astype(BF16)
    hl = (h2 - hh.astype(F32)).astype(BF16)
    logits = _dot(hh, wrh_ref[...]) + (_dot(hh, wrl_ref[...]) + _dot(hl, wrh_ref[...])) + br_ref[...]
    lane = lax.broadcasted_iota(jnp.int32, logits.shape, 1)
    lane_f = lane.astype(F32)
    is_grp = lane < N_GROUPS
    lg = jnp.where(is_grp, logits, NEG_INF)
    gmax = jnp.max(lg, axis=-1, keepdims=True)
    gidx = jnp.min(jnp.where(lg == gmax, lane_f, float(LANES)), axis=-1, keepdims=True)
    p_grp = 1.0 / jnp.sum(jnp.where(is_grp, jnp.exp(lg - gmax), 0.0), axis=-1, keepdims=True)
    egrp = ((lane - N_GROUPS) // EXPERTS_PER_GROUP).astype(F32)
    in_grp = (lane >= N_GROUPS) & (lane < N_GROUPS + N_EXPERTS) & (egrp == gidx)
    le = jnp.where(in_grp, logits, REMOVED)
    v1 = jnp.max(le, axis=-1, keepdims=True)
    i1 = jnp.min(jnp.where(le == v1, lane_f, float(LANES)), axis=-1, keepdims=True)
    le2 = jnp.where(lane_f == i1, REMOVED, le)
    v2 = jnp.max(le2, axis=-1, keepdims=True)
    i2 = jnp.min(jnp.where(le2 == v2, lane_f, float(LANES)), axis=-1, keepdims=True)
    e2 = jnp.exp(v2 - v1)
    w1 = 1.0 / (1.0 + e2)
    cw_ref[...] = jnp.where(lane_f == i1, w1 * p_grp, jnp.where(lane_f == i2, e2 * w1 * p_grp, 0.0))


def _merge(oa, ob, gm, x, g1, sc2, sh2, per_token, rows_per_seq, wp):
    n, d = x.shape
    tm = min(ROW_TILE, n)
    bps = max(rows_per_seq // tm, 1)
    mod = _mod_spec(per_token, tm, bps, d)

    def row(width):
        return pl.BlockSpec((tm, width), lambda i: (i, 0))

    return pl.pallas_call(
        _merge_kernel,
        out_shape=(jax.ShapeDtypeStruct((n, d), F32), jax.ShapeDtypeStruct((n, d), BF16),
                   jax.ShapeDtypeStruct((n, LANES), F32)),
        grid=(n // tm,),
        in_specs=[row(HA * DVA), row(HB * DB), row(2 * d), row(d), mod, mod, mod, _const_spec((1, d)),
                  _const_spec((HA * DVA, d)), _const_spec((HB * DB, d)), _const_spec((d, d)),
                  _const_spec((d, LANES)), _const_spec((d, LANES)), _const_spec((1, LANES))],
        out_specs=(row(d), row(d), row(LANES)),
        compiler_params=_cparams(("parallel",)),
        name="merge",
    )(oa, ob, gm, x, g1, sc2, sh2, wp["norm2_g"], wp["w_up_a"], wp["w_up_b"], wp["w_out"],
      wp["w_r_hi"], wp["w_r_lo"], wp["b_r"])


def _moe_kernel(h_ref, cw_ref, x_ref, g2_ref, wg_ref, wu_ref, wd_ref, o_ref, acc_ref):
    grp = pl.program_id(1)

    @pl.when(grp == 0)
    def _():
        acc_ref[...] = jnp.zeros_like(acc_ref)

    lane = lax.broadcasted_iota(jnp.int32, cw_ref.shape, 1)
    cw_all = cw_ref[...]
    h = h_ref[...]
    a = _dot(h, wg_ref[...])
    u = _dot(h, wu_ref[...])
    cols = []
    for el in range(EXPERTS_PER_GROUP):
        cw = jnp.sum(jnp.where(lane == N_GROUPS + grp * EXPERTS_PER_GROUP + el, cw_all, 0.0), axis=-1, keepdims=True)
        ae = a[:, el * D_EXPERT:(el + 1) * D_EXPERT]
        cols.append((ae * jax.nn.sigmoid(ae) * u[:, el * D_EXPERT:(el + 1) * D_EXPERT] * cw).astype(BF16))
    acc_ref[...] += _dot(jnp.concatenate(cols, axis=1), wd_ref[...])

    @pl.when(grp == pl.num_programs(1) - 1)
    def _():
        o_ref[...] = x_ref[...] + g2_ref[...] * acc_ref[...]


def _moe(h2, cw, x1, g2, per_token, rows_per_seq, wp):
    n, d = x1.shape
    tm = min(MOE_ROW_TILE, n)
    bps = max(rows_per_seq // tm, 1)
    gw = EXPERTS_PER_GROUP * D_EXPERT
    if per_token:
        mod = pl.BlockSpec((tm, d), lambda i, e: (i, 0))
    else:
        mod = pl.BlockSpec((None, 1, d), lambda i, e: (i // bps, 0, 0))
    return pl.pallas_call(
        _moe_kernel,
        out_shape=jax.ShapeDtypeStruct((n, d), F32),
        grid=(n // tm, N_GROUPS),
        in_specs=[pl.BlockSpec((tm, d), lambda i, e: (i, 0)), pl.BlockSpec((tm, LANES), lambda i, e: (i, 0)),
                  pl.BlockSpec((tm, d), lambda i, e: (i, 0)), mod,
                  pl.BlockSpec((None, d, gw), lambda i, e: (e, 0, 0)),
                  pl.BlockSpec((None, d, gw), lambda i, e: (e, 0, 0)),
                  pl.BlockSpec((None, gw, d), lambda i, e: (e, 0, 0))],
        out_specs=pl.BlockSpec((tm, d), lambda i, e: (i, 0)),
        scratch_shapes=[pltpu.VMEM((tm, d), F32)],
        compiler_params=_cparams(("parallel", "arbitrary")),
        name="moe",
    )(h2, cw, x1, g2, wp["w_gate"], wp["w_up"], wp["w_down"])


def _page_specs(n_pages, ppc, rows, width):
    def spec(p):
        return pl.BlockSpec((None, rows, width), lambda s, c, pt: (pt[s * n_pages + c * ppc + p], 0, 0))
    return [spec(p) for p in range(ppc)]


def _pad_rows(x, rows):
    return jnp.concatenate([x, jnp.zeros((rows - x.shape[0], x.shape[1]), x.dtype)], axis=0)


def _online_update(m_ref, l_ref, acc_ref, s, valid, pv_fn):
    if valid is not None:
        s = jnp.where(valid, s, NEG_INF)
    m_old = m_ref[...]
    m_new = jnp.maximum(m_old, jnp.max(s, axis=-1, keepdims=True))
    alpha = jnp.exp(m_old - m_new)
    p = jnp.exp(s - m_new)
    if valid is not None:
        p = jnp.where(valid, p, 0.0)
    m_ref[...] = m_new
    l_ref[...] = alpha * l_ref[...] + jnp.sum(p, axis=-1, keepdims=True)
    acc_ref[...] = alpha * acc_ref[...] + pv_fn(p.astype(BF16))


def _diff_sample_kernel(pt_ref, lam_ref, q_ref, kn_ref, vn_ref, hn_ref, *rest, ppc, page, past_len, n_new, out_scale):
    k_refs, v_refs = rest[:ppc], rest[ppc:2 * ppc]
    o_ref, m_ref, l_ref, acc_ref = rest[2 * ppc:]
    c = pl.program_id(1)
    rows = HA * 2 * 8
    hrows = 2 * 8
    lam = lam_ref[0]

    @pl.when(c == 0)
    def _():
        m_ref[...] = jnp.full(m_ref.shape, NEG_INF, F32)
        l_ref[...] = jnp.zeros_like(l_ref)
        acc_ref[...] = jnp.zeros_like(acc_ref)

    ridx = lax.broadcasted_iota(jnp.int32, (rows, 1), 0)
    qpos = past_len + ridx % 8
    slope = jnp.zeros((rows, 1), F32)
    for h in range(HA):
        slope = jnp.where(ridx // hrows == h, SLOPES_A[h], slope)
    q = q_ref[...]

    def per_head(p, v_heads):
        return jnp.concatenate([_dot(p[h * hrows:(h + 1) * hrows], v_heads[h]) for h in range(HA)], axis=0)

    chunk = ppc * page
    kt = jnp.concatenate([r[...] for r in k_refs], axis=1).astype(BF16)
    v_heads = [jnp.concatenate([r[pl.ds(h, page, stride=HA), :] for r in v_refs], axis=0).astype(BF16)
               for h in range(HA)]
    kpos = c * chunk + lax.broadcasted_iota(jnp.int32, (1, chunk), 1)
    s = _dot(q, kt) - slope * (qpos - kpos).astype(F32)
    _online_update(m_ref, l_ref, acc_ref, s, None, lambda p: per_head(p, v_heads))

    @pl.when(c == pl.num_programs(1) - 1)
    def _():
        kn = _pad_rows(kn_ref[...], LANES).astype(BF16)
        vn = _pad_rows(vn_ref[...], LANES).astype(BF16)
        vn_heads = [vn[:, h * DVA:(h + 1) * DVA] for h in range(HA)]
        npos = past_len + lax.broadcasted_iota(jnp.int32, (1, LANES), 1)
        s_new = _dot_nt(q, kn) - slope * (qpos - npos).astype(F32)
        _online_update(m_ref, l_ref, acc_ref, s_new, (npos <= qpos) & (npos < past_len + n_new),
                       lambda p: per_head(p, vn_heads))
        o_all = acc_ref[...] / l_ref[...]
        for h in range(HA):
            r0 = h * hrows
            o = o_all[r0:r0 + 8] - lam * o_all[r0 + 8:r0 + 16]
            o = o * lax.rsqrt(jnp.mean(o * o, axis=-1, keepdims=True) + RMS_EPS) * hn_ref[...]
            o_ref[:, h * DVA:(h + 1) * DVA] = (o * out_scale).astype(BF16)


def _diff_sample(pt, lam, qbd, k_new, v_new, hn, pool_kt, pool_v, n_pages, past_len, n_new, out_scale):
    s = qbd.shape[0]
    page = pool_kt.shape[2]
    ppc = math.gcd(PAGES_PER_STEP, n_pages)
    rows = HA * 2 * 8
    per_seq = lambda shape: pl.BlockSpec((None,) + shape, lambda si, c, pt: (si, 0, 0))
    return pl.pallas_call(
        functools.partial(_diff_sample_kernel, ppc=ppc, page=page, past_len=past_len, n_new=n_new, out_scale=out_scale),
        out_shape=jax.ShapeDtypeStruct((s, 8, HA * DVA), BF16),
        grid_spec=pltpu.PrefetchScalarGridSpec(
            num_scalar_prefetch=1, grid=(s, n_pages // ppc),
            in_specs=[pl.BlockSpec(memory_space=pltpu.SMEM), per_seq((rows, HA * 2 * DA)),
                      per_seq((8, HA * 2 * DA)), per_seq((8, HA * DVA)), _const_spec((1, DVA))]
                     + _page_specs(n_pages, ppc, HA * 2 * DA, page) + _page_specs(n_pages, ppc, page * HA, DVA),
            out_specs=per_seq((8, HA * DVA)),
            scratch_shapes=[pltpu.VMEM((rows, 1), F32), pltpu.VMEM((rows, 1), F32), pltpu.VMEM((rows, DVA), F32)]),
        compiler_params=_cparams(("parallel", "arbitrary")),
        name="diff_sample",
    )(pt, lam, qbd, k_new, v_new, hn, *([pool_kt] * ppc), *([pool_v] * ppc))


def _cmp_sample_kernel(pt_ref, q_ref, kn_ref, vn_ref, w1k_ref, pek_ref, b1k_ref, w2k_ref, b2k_ref,
                       w1v_ref, pev_ref, b1v_ref, w2v_ref, b2v_ref, kng_ref, ones_ref, gsum_ref, *rest,
                       ppc, page, past_len, m_rows, mc_rows, nsb):
    k_refs, v_refs = rest[:ppc], rest[ppc:2 * ppc]
    o_ref, sel_ref, bufk, bufv = rest[2 * ppc:]
    c = pl.program_id(1)

    @pl.when(c == 0)
    def _():
        tail = bufk.shape[0] - past_len
        for buf, new in ((bufk, kn_ref), (bufv, vn_ref)):
            buf[pl.ds(past_len, tail), :] = jnp.zeros((tail, LANES), F32)
            buf[pl.ds(past_len, 8), :] = new[...]

    base = c * (ppc * page)
    for p in range(ppc):
        start = pl.multiple_of(base + p * page, page)
        bufk[pl.ds(start, page), :] = k_refs[p][...]
        bufv[pl.ds(start, page), :] = v_refs[p][...]

    @pl.when(c == pl.num_programs(1) - 1)
    def _():
        pad = jnp.zeros((m_rows - mc_rows, LANES), F32)
        k = _compress(bufk, mc_rows, w1k_ref, pek_ref, b1k_ref, w2k_ref, b2k_ref)
        k = _group_norm(k, kng_ref[...], ones_ref[...])
        kcmp_b = jnp.concatenate([k, pad], axis=0).astype(BF16)
        v = _compress(bufv, mc_rows, w1v_ref, pev_ref, b1v_ref, w2v_ref, b2v_ref)
        vcmp_b = jnp.concatenate([v, pad], axis=0).astype(BF16)
        rows = HB * 8
        ridx = lax.broadcasted_iota(jnp.int32, (rows, 1), 0)
        qpos = past_len + ridx % 8
        slope = jnp.zeros((rows, 1), F32)
        for h in range(HB):
            slope = jnp.where(ridx // 8 == h, SLOPES_B[h], slope)
        cend = lax.broadcasted_iota(jnp.int32, (1, m_rows), 1) * CMP_STRIDE + (L_CMP - 1)
        visible = cend <= qpos
        s = _dot_nt(q_ref[...], kcmp_b) - slope * (qpos - cend).astype(F32)
        s = jnp.where(visible, s, NEG_INF)
        e = jnp.where(visible, jnp.exp(s - jnp.max(s, axis=-1, keepdims=True)), 0.0)
        p = e / jnp.maximum(jnp.sum(e, axis=-1, keepdims=True), 1e-30)
        o_ref[...] = _dot(p.astype(BF16), vcmp_b)
        gsum = gsum_ref[...]
        for kvh in range(KVH):
            psum = p[kvh * GB * 8:kvh * GB * 8 + 8]
            for g in range(1, GB):
                psum = psum + p[(kvh * GB + g) * 8:(kvh * GB + g + 1) * 8]
            sel_ref[kvh] = _select_blocks(psum, qpos[:8], gsum, nsb)


def _cmp_sample(pt, q_heads, k_new, v_new, pool_k, pool_v, n_pages, past_len, n_new, wp):
    s = q_heads.shape[0]
    page = pool_k.shape[1]
    ppc = math.gcd(PAGES_PER_STEP, n_pages)
    tk_pad = _round_up(past_len + n_new, SEL_BLOCK)
    nc = tk_pad // CMP_STRIDE
    nsb = tk_pad // SEL_BLOCK
    m_rows = _round_up(nc, LANES)
    nsbp = _round_up(nsb, LANES)
    mc_rows = _round_up(nc, 8)
    buf_rows = CMP_STRIDE * (mc_rows - 1) + L_CMP
    gsum = ((jnp.arange(m_rows)[:, None] // (SEL_BLOCK // CMP_STRIDE) == jnp.arange(nsbp)[None, :])
            & (jnp.arange(m_rows)[:, None] < nc)).astype(BF16)
    cw = [_const_spec((L_CMP * LANES, KVH * CMP_HID)), _const_spec((1, L_CMP * LANES)), _const_spec((1, KVH * CMP_HID)),
          _const_spec((KVH * CMP_HID, LANES)), _const_spec((1, LANES))]
    per_seq = lambda shape: pl.BlockSpec((None,) + shape, lambda si, c, pt: (si,) + (0,) * len(shape))
    return pl.pallas_call(
        functools.partial(_cmp_sample_kernel, ppc=ppc, page=page, past_len=past_len, m_rows=m_rows, mc_rows=mc_rows,
                          nsb=nsb),
        out_shape=(jax.ShapeDtypeStruct((s, HB * 8, LANES), F32), jax.ShapeDtypeStruct((s, KVH, 8, nsbp), F32)),
        grid_spec=pltpu.PrefetchScalarGridSpec(
            num_scalar_prefetch=1, grid=(s, n_pages // ppc),
            in_specs=[per_seq((HB * 8, LANES)), per_seq((8, LANES)), per_seq((8, LANES))] + cw + cw
                     + [_const_spec((1, LANES)), _const_spec((256, 256)), _const_spec((m_rows, nsbp))]
                     + _page_specs(n_pages, ppc, page, LANES) + _page_specs(n_pages, ppc, page, LANES),
            out_specs=(per_seq((HB * 8, LANES)), per_seq((KVH, 8, nsbp))),
            scratch_shapes=[pltpu.VMEM((buf_rows, LANES), F32), pltpu.VMEM((buf_rows, LANES), F32)]),
        compiler_params=_cparams(("parallel", "arbitrary")),
        name="cmp_sample",
    )(pt, q_heads, k_new, v_new, *wp["cmp_k"], *wp["cmp_v"], wp["g_kc"], wp["ones"], gsum,
      *([pool_k] * ppc), *([pool_v] * ppc))


def _nsa_sample_kernel(pt_ref, q_ref, sel_ref, e_ref, kn_ref, vn_ref, kwin_ref, vwin_ref, kwn_ref, vwn_ref,
                       ocmp_ref, gate_ref, *rest, ppc, page, past_len, n_new, nsb, w_buf):
    k_refs, v_refs = rest[:ppc], rest[ppc:2 * ppc]
    o_ref, m_ref, l_ref, acc_ref, mw_ref, lw_ref, accw_ref = rest[2 * ppc:]
    c = pl.program_id(1)
    rows = HB * 8

    @pl.when(c == 0)
    def _():
        m_ref[...] = jnp.full(m_ref.shape, NEG_INF, F32)
        l_ref[...] = jnp.zeros_like(l_ref)
        acc_ref[...] = jnp.zeros_like(acc_ref)

    ridx = lax.broadcasted_iota(jnp.int32, (rows, 1), 0)
    qpos = past_len + ridx % 8
    slope = jnp.zeros((rows, 1), F32)
    for h in range(HB):
        slope = jnp.where(ridx // 8 == h, SLOPES_B[h], slope)
    q = q_ref[...]
    selm = jnp.concatenate([sel_ref[kvh] for kvh in range(KVH) for _ in range(GB)], axis=0)

    def bias(kpos):
        return slope * (qpos - kpos).astype(F32)

    chunk = ppc * page
    kt = jnp.concatenate([r[...] for r in k_refs], axis=1).astype(BF16)
    vt = jnp.concatenate([r[...] for r in v_refs], axis=1).astype(BF16)
    kpos = c * chunk + lax.broadcasted_iota(jnp.int32, (1, chunk), 1)
    mask = _dot(selm.astype(BF16), e_ref[...])
    _online_update(m_ref, l_ref, acc_ref, _dot(q, kt) - bias(kpos), mask > 0.5, lambda p: _dot_nt(p, vt))

    @pl.when(c == pl.num_programs(1) - 1)
    def _():
        npos = past_len + lax.broadcasted_iota(jnp.int32, (1, LANES), 1)
        is_new = (npos <= qpos) & (npos < past_len + n_new)
        kn = _pad_rows(kn_ref[...], LANES).astype(BF16)
        vn = _pad_rows(vn_ref[...], LANES).astype(BF16)
        blk_new = past_len // SEL_BLOCK
        lane_s = lax.broadcasted_iota(jnp.int32, selm.shape, 1)
        picked = jnp.sum(jnp.where(lane_s == blk_new, selm, 0.0), axis=-1, keepdims=True) > 0.5
        _online_update(m_ref, l_ref, acc_ref, _dot_nt(q, kn) - bias(npos), picked & is_new, lambda p: _dot(p, vn))
        o_sel = acc_ref[...] / jnp.maximum(l_ref[...], 1e-30)
        mw_ref[...] = jnp.full(mw_ref.shape, NEG_INF, F32)
        lw_ref[...] = jnp.zeros_like(lw_ref)
        accw_ref[...] = jnp.zeros_like(accw_ref)
        kwt = kwin_ref[...].astype(BF16)
        vwt = vwin_ref[...].astype(BF16)
        wpos = past_len - w_buf + lax.broadcasted_iota(jnp.int32, (1, w_buf), 1)
        dist = qpos - wpos
        valid = (dist >= 0) & (dist < WINDOW) & (wpos >= 0)
        _online_update(mw_ref, lw_ref, accw_ref, _dot(q, kwt) - bias(wpos), valid, lambda p: _dot_nt(p, vwt))
        kwn = _pad_rows(kwn_ref[...], LANES).astype(BF16)
        vwn = _pad_rows(vwn_ref[...], LANES).astype(BF16)
        _online_update(mw_ref, lw_ref, accw_ref, _dot_nt(q, kwn) - bias(npos), is_new & (qpos - npos < WINDOW),
                       lambda p: _dot(p, vwn))
        o_win = accw_ref[...] / jnp.maximum(lw_ref[...], 1e-30)
        g = gate_ref[...]
        o_ref[...] = g[:, 0:1] * ocmp_ref[...] + g[:, 1:2] * o_sel + g[:, 2:3] * o_win


def _nsa_sample(pt, qbd, sel, k_new, v_new, kwin_t, vwin_t, kw_new, vw_new, ocmp, gates, pool_kt, pool_vt,
                n_pages, past_len, n_new, layer):
    s = qbd.shape[0]
    page = pool_kt.shape[2]
    w_buf = kwin_t.shape[2]
    ppc = math.gcd(PAGES_PER_STEP, n_pages)
    nsbp = sel.shape[-1]
    nsb = _round_up(past_len + n_new, SEL_BLOCK) // SEL_BLOCK
    rows = HB * 8
    emat = (jnp.arange(past_len)[None, :] // SEL_BLOCK == jnp.arange(nsbp)[:, None]).astype(BF16)
    per_seq = lambda shape: pl.BlockSpec((None,) + shape, lambda si, c, pt: (si,) + (0,) * len(shape))
    state = pl.BlockSpec((None, LANES, w_buf), lambda si, c, pt: (layer * s + si, 0, 0))
    scratch = [pltpu.VMEM((rows, 1), F32), pltpu.VMEM((rows, 1), F32), pltpu.VMEM((rows, LANES), F32)]
    return pl.pallas_call(
        functools.partial(_nsa_sample_kernel, ppc=ppc, page=page, past_len=past_len, n_new=n_new, nsb=nsb, w_buf=w_buf),
        out_shape=jax.ShapeDtypeStruct((s, rows, LANES), F32),
        grid_spec=pltpu.PrefetchScalarGridSpec(
            num_scalar_prefetch=1, grid=(s, n_pages // ppc),
            in_specs=[per_seq((rows, LANES)), per_seq((KVH, 8, nsbp)),
                      pl.BlockSpec((nsbp, ppc * page), lambda si, c, pt: (0, c)),
                      per_seq((8, LANES)), per_seq((8, LANES)), state, state, per_seq((8, LANES)), per_seq((8, LANES)),
                      per_seq((rows, LANES)), per_seq((rows, LANES))]
                     + _page_specs(n_pages, ppc, LANES, page) + _page_specs(n_pages, ppc, LANES, page),
            out_specs=per_seq((rows, LANES)),
            scratch_shapes=scratch + scratch),
        compiler_params=_cparams(("parallel", "arbitrary")),
        name="nsa_sample",
    )(pt, qbd, sel, emat, k_new, v_new, kwin_t, vwin_t, kw_new, vw_new, ocmp, gates,
      *([pool_kt] * ppc), *([pool_vt] * ppc))


def _prep_cmp(w1, pe, b1, w2, b2):
    eye = jnp.eye(KVH, dtype=F32)
    w1b = jnp.einsum('jde,hk->jhdke', w1, eye).reshape(L_CMP * KVH * DB, KVH * CMP_HID).astype(BF16)
    pef = jnp.tile(pe[:, None, :], (1, KVH, 1)).reshape(1, L_CMP * KVH * DB)
    w2b = jnp.einsum('ed,hk->hekd', w2, eye).reshape(KVH * CMP_HID, KVH * DB).astype(BF16)
    return (w1b, pef, jnp.tile(b1, KVH).reshape(1, -1), w2b, jnp.tile(b2, KVH).reshape(1, -1))


def _prep_layer(l, w_in, norm1_g, norm2_g, qn_a, kn_a, hn_a, qn_b, kn_cmp, kn_sel, kn_win, cmp_k, cmp_v,
                w_up_a, w_up_b, w_out, w_group, b_group, w_router, b_router, w_e_gate, w_e_up, w_e_down):
    d = w_in.shape[1]
    wi = w_in[l]
    o_qb = 2 * HA * 2 * DA + HA * DVA
    o_kv = o_qb + HB * DB
    o_gb = o_kv + 6 * KVH * DB
    o_gm = o_gb + 3 * HB
    wqb = wi[:, o_qb:o_kv].reshape(d, KVH, GB, DB)
    slot = jnp.zeros((d, KVH, GB, KVH, DB), F32)
    for kvh in range(KVH):
        slot = slot.at[:, kvh, :, kvh, :].set(wqb[:, kvh])
    wqb_ext = slot.reshape(d, HB * LANES)
    wgb = jnp.pad(wi[:, o_gb:o_gm], ((0, 0), (0, LANES - 3 * HB)))
    w_all = jnp.concatenate([wi[:, :o_qb], wqb_ext, wi[:, o_kv:o_gb], wgb, wi[:, o_gm:]], axis=1).astype(BF16)
    assert w_all.shape[1] == _W_COLS
    gq = jnp.zeros((KVH, GB, KVH, DB), F32)
    for kvh in range(KVH):
        gq = gq.at[kvh, :, kvh, :].set(jnp.broadcast_to(qn_b[l], (GB, DB)))
    ones = (jnp.arange(256)[:, None] // 64 == jnp.arange(256)[None, :] // 64).astype(BF16)
    w_r = jnp.concatenate([w_group[l], w_router[l].reshape(d, N_EXPERTS)], axis=1)
    w_r = jnp.pad(w_r, ((0, 0), (0, LANES - w_r.shape[1])))
    w_r_hi = w_r.astype(BF16)

    def by_group(w):
        w = w.reshape(N_GROUPS, EXPERTS_PER_GROUP, d, D_EXPERT).transpose(0, 2, 1, 3)
        return w.reshape(N_GROUPS, d, EXPERTS_PER_GROUP * D_EXPERT).astype(BF16)
    b_r = jnp.pad(jnp.concatenate([b_group[l], b_router[l].reshape(-1)]), (0, LANES - N_GROUPS - N_EXPERTS))
    return {
        "norm1_g": norm1_g[l].reshape(1, d), "norm2_g": norm2_g[l].reshape(1, d), "w_in": w_all,
        "g_qa": jnp.tile(qn_a[l], HA * 2).reshape(1, -1), "g_ka": jnp.tile(kn_a[l], HA * 2).reshape(1, -1),
        "g_qb": gq.reshape(1, -1), "g_ks": jnp.tile(kn_sel[l], KVH).reshape(1, -1),
        "g_kw": jnp.tile(kn_win[l], KVH).reshape(1, -1), "g_kc": jnp.tile(kn_cmp[l], KVH).reshape(1, -1),
        "hn_a": hn_a[l].reshape(1, DVA), "ones": ones,
        "cmp_k": _prep_cmp(*[a[l] for a in cmp_k]), "cmp_v": _prep_cmp(*[a[l] for a in cmp_v]),
        "w_up_a": w_up_a[l].astype(BF16), "w_up_b": w_up_b[l].astype(BF16), "w_out": w_out[l].astype(BF16),
        "w_r_hi": w_r_hi, "w_r_lo": (w_r - w_r_hi.astype(F32)).astype(BF16), "b_r": b_r.reshape(1, LANES),
        "w_gate": by_group(w_e_gate[l]), "w_up": by_group(w_e_up[l]),
        "w_down": w_e_down[l].reshape(N_GROUPS, EXPERTS_PER_GROUP * D_EXPERT, d).astype(BF16),
    }


def kernel(x_prompt, x_sample, cache_diff_k, cache_diff_v, cache_cmp_k, cache_cmp_v, cache_sel_k, cache_sel_v, state_win_k, state_win_v, page_table, c_prompt, c_sample, w_ada, b_ada, norm1_g, norm2_g, w_in, qn_a, kn_a, hn_a, lam_q1, lam_k1, lam_q2, lam_k2, qn_b, kn_cmp, kn_sel, kn_win, cmp_pe_k, cmp_w1_k, cmp_b1_k, cmp_w2_k, cmp_b2_k, cmp_pe_v, cmp_w1_v, cmp_b1_v, cmp_w2_v, cmp_b2_v, w_up_a, w_up_b, w_out, w_group, b_group, w_router, b_router, w_e_gate, w_e_up, w_e_down):
    b, t, d = x_prompt.shape
    s, n_new, _ = x_sample.shape
    depth, n_phys, page = cache_diff_k.shape[:3]
    n_pages = page_table.shape[1]
    past_len = n_pages * page
    w_buf = state_win_k.shape[2]
    assert n_new <= 8 and past_len % SEL_BLOCK == 0 and t % LANES == 0

    n_flat = depth * n_phys

    def page_t(p):
        nd = p.ndim
        return p.transpose(0, 1, *range(3, nd), 2).reshape(n_flat, -1, page)

    pool_diff_kt = page_t(cache_diff_k)
    pool_diff_v = cache_diff_v.reshape(n_flat, page * HA, DVA)
    pool_cmp_k = cache_cmp_k.reshape(n_flat, page, KVH * DB)
    pool_cmp_v = cache_cmp_v.reshape(n_flat, page, KVH * DB)
    pool_sel_kt, pool_sel_vt = page_t(cache_sel_k), page_t(cache_sel_v)
    win_kt = state_win_k.transpose(0, 1, 3, 4, 2).reshape(depth * s, KVH * DB, w_buf)
    win_vt = state_win_v.transpose(0, 1, 3, 4, 2).reshape(depth * s, KVH * DB, w_buf)
    slopes_a = jnp.asarray(SLOPES_A, F32)

    xp = x_prompt.reshape(b * t, d)
    xs = x_sample.reshape(s * n_new, d)
    c_all = jnp.concatenate([c_prompt, c_sample], axis=0)
    new_p = [[] for _ in range(8)]
    new_s = [[] for _ in range(8)]
    for l in range(depth):
        lambda_init = 0.8 - 0.6 * math.exp(-0.3 * l)
        out_scale = 1.0 - lambda_init
        wp = _prep_layer(l, w_in, norm1_g, norm2_g, qn_a, kn_a, hn_a, qn_b, kn_cmp, kn_sel, kn_win,
                         (cmp_w1_k, cmp_pe_k, cmp_b1_k, cmp_w2_k, cmp_b2_k),
                         (cmp_w1_v, cmp_pe_v, cmp_b1_v, cmp_w2_v, cmp_b2_v),
                         w_up_a, w_up_b, w_out, w_group, b_group, w_router, b_router, w_e_gate, w_e_up, w_e_down)
        mod, lam_row = _ada(c_all, w_ada[l], b_ada[l], lam_q1[l], lam_k1[l], lam_q2[l], lam_k2[l], lambda_init)
        lam = lam_row[0, :1]
        mods_p = [m.reshape(b, 1, d) for m in jnp.split(mod[:b], 6, axis=-1)]
        mods_s = [jnp.repeat(m, n_new, axis=0) for m in jnp.split(mod[b:], 6, axis=-1)]

        (qa, ka, va, qb, kc, vc, ks, vs, kw, vw, gbs, gm) = _proj(xp, mods_p[1], mods_p[0], False, t, wp)
        r3 = lambda a: a.reshape(b, t, -1)
        oa = _diff_prompt(r3(qa), r3(ka), r3(va), slopes_a, lam, wp["hn_a"], out_scale)
        ocmp, sel = _cmp_prompt(r3(qb), r3(kc), r3(vc), wp)
        ob = _nsa_prompt(r3(qb), r3(ks), r3(vs), r3(kw), r3(vw), sel, ocmp, r3(gbs))
        x1, h2, cw = _merge(oa.reshape(b * t, -1), ob.reshape(b * t, -1), gm, xp, mods_p[2], mods_p[4], mods_p[3],
                            False, t, wp)
        xp = _moe(h2, cw, x1, mods_p[5], False, t, wp)
        wp_rows = min(WINDOW, t)
        rows_p = [ka.reshape(b, t, HA, 2, DA), va.reshape(b, t, HA, DVA)] + \
                 [a.reshape(b, t, KVH, DB) for a in (kc, vc, ks, vs)] + \
                 [a.reshape(b, t, KVH, DB)[:, -wp_rows:] for a in (kw, vw)]

        (qa, ka, va, qb, kc, vc, ks, vs, kw, vw, gbs, gm) = _proj(xs, mods_s[1], mods_s[0], True, n_new, wp)
        pt = (page_table + l * n_phys).reshape(-1).astype(jnp.int32)
        pad8 = lambda a: jnp.pad(a.reshape(s, n_new, -1), ((0, 0), (0, 8 - n_new), (0, 0)))
        qa8 = pad8(qa)
        own = (jnp.arange(HA * 2 * DA)[None, :] // DA == jnp.arange(HA * 2)[:, None]).astype(BF16)
        qbd = (qa8[:, None, :, :] * own[None, :, None, :]).reshape(s, HA * 2 * 8, HA * 2 * DA)
        oa_s = _diff_sample(pt, lam, qbd, pad8(ka), pad8(va), wp["hn_a"], pool_diff_kt, pool_diff_v,
                            n_pages, past_len, n_new, out_scale)
        oa_s = oa_s[:, :n_new].reshape(s * n_new, HA * DVA)
        q_heads = pad8(qb).reshape(s, 8, HB, LANES).transpose(0, 2, 1, 3)
        ocmp_s, sel_s = _cmp_sample(pt, q_heads.reshape(s, HB * 8, LANES), pad8(kc), pad8(vc), pool_cmp_k, pool_cmp_v, n_pages, past_len, n_new, wp)
        kwin = jnp.concatenate([state_win_k[l].reshape(s, w_buf, -1), kw.reshape(s, n_new, -1)], axis=1)
        vwin = jnp.concatenate([state_win_v[l].reshape(s, w_buf, -1), vw.reshape(s, n_new, -1)], axis=1)
        gates = pad8(gbs)[:, :, :3 * HB].reshape(s, 8, HB, 3).transpose(0, 2, 1, 3).reshape(s, HB * 8, 3)
        gates = jnp.pad(gates, ((0, 0), (0, 0), (0, LANES - 3)))
        ob_s = _nsa_sample(pt, q_heads.reshape(s, HB * 8, LANES), sel_s, pad8(ks), pad8(vs), win_kt, win_vt,
                           pad8(kw), pad8(vw), ocmp_s.reshape(s, HB * 8, LANES), gates, pool_sel_kt, pool_sel_vt,
                           n_pages, past_len, n_new, l)
        ob_s = ob_s.reshape(s, KVH, GB, 8, KVH, DB)
        ob_s = jnp.stack([ob_s[:, kvh, :, :, kvh, :] for kvh in range(KVH)], axis=1)
        ob_s = ob_s.transpose(0, 3, 1, 2, 4)[:, :n_new].reshape(s * n_new, HB * DB).astype(BF16)
        x1, h2, cw = _merge(oa_s, ob_s, gm, xs, mods_s[2], mods_s[4], mods_s[3], True, n_new, wp)
        xs = _moe(h2, cw, x1, mods_s[5], True, n_new, wp)
        rows_s = [ka.reshape(s, n_new, HA, 2, DA), va.reshape(s, n_new, HA, DVA)] + \
                 [a.reshape(s, n_new, KVH, DB) for a in (kc, vc, ks, vs)] + \
                 [kwin[:, -w_buf:].reshape(s, w_buf, KVH, DB), vwin[:, -w_buf:].reshape(s, w_buf, KVH, DB)]
        for i in range(8):
            new_p[i].append(rows_p[i])
            new_s[i].append(rows_s[i])

    outs_p = [jnp.stack(a) for a in new_p]
    outs_s = [jnp.stack(a) for a in new_s]
    return (xp.reshape(b, t, d), xs.reshape(s, n_new, d), *outs_p, *outs_s)
```

```python
import functools
import math

import jax
import jax.numpy as jnp
from jax import lax
from jax.experimental import pallas as pl
from jax.experimental.pallas import tpu as pltpu

F32 = jnp.float32
BF16 = jnp.bfloat16

HA, DA, DVA = 4, 64, 128
HB, KVH, GB, DB = 8, 2, 4, 64
L_CMP, CMP_STRIDE, CMP_HID = 32, 16, 128
SEL_BLOCK, N_SEL, WINDOW = 64, 8, 512
N_GROUPS, EXPERTS_PER_GROUP, N_EXPERTS, D_EXPERT = 4, 4, 16, 256
RMS_EPS = 1e-6
NEG_INF = -1e30
BIG = 1e9
REMOVED = -3e38
MASK_NEG = 1e30
AUG_POS = DB - 2
SLOPES_A = tuple(2.0 ** (-8.0 * (i + 1) / HA) for i in range(HA))
SLOPES_B = tuple(2.0 ** (-8.0 * (i + 1) / HB) for i in range(HB))
QK_SCALE = 0.125

LANES = 128
VMEM_LIMIT = 56 * 1024 * 1024

ROW_TILE = 256
MOE_ROW_TILE = 512
DIFF_TQ = 256
NSA_TQ = 128
NSA_TK = 256
CMP_TQ = 256
DIFF_PAGES_PER_STEP = 16
NSA_PAGES_PER_STEP = 32


def _cparams(sem):
    return pltpu.CompilerParams(dimension_semantics=sem, vmem_limit_bytes=VMEM_LIMIT)


def _const_spec(shape):
    nd = len(shape)
    return pl.BlockSpec(shape, lambda *_: (0,) * nd)


def _round_up(x, m):
    return -(-x // m) * m


def _split3(x):
    x1 = x.astype(BF16)
    r = x - x1.astype(F32)
    x2 = r.astype(BF16)
    x3 = (r - x2.astype(F32)).astype(BF16)
    return x1, x2, x3


def _dot(a, b):
    return jnp.dot(a, b, preferred_element_type=F32)


def _dot_nt(a, b):
    return lax.dot_general(a, b, (((1,), (1,)), ((), ())), preferred_element_type=F32)


def _gelu_tanh(x):
    return 0.5 * x * (1.0 + jnp.tanh(0.7978845608028654 * (x + 0.044715 * x * x * x)))


def _topk_mask(imp, k):
    width = imp.shape[-1]
    lane = lax.broadcasted_iota(jnp.int32, imp.shape, 1).astype(F32)
    sel = jnp.zeros(imp.shape, F32)
    work = imp
    for _ in range(k):
        m = jnp.max(work, axis=-1, keepdims=True)
        idx = jnp.min(jnp.where(work == m, lane, float(width)), axis=-1, keepdims=True)
        hit = lane == idx
        sel = jnp.where(hit, 1.0, sel)
        work = jnp.where(hit, REMOVED, work)
    return sel


def _ada_kernel(c_ref, w_ref, b_ref, q1_ref, k1_ref, q2_ref, k2_ref, o_ref, lam_ref, *, lambda_init):
    c = c_ref[...]
    s = c * jax.nn.sigmoid(c)
    s1, s2, s3 = _split3(s)
    w1, w2, w3 = _split3(w_ref[...])
    acc = _dot(s1, w1) + (_dot(s1, w2) + _dot(s2, w1)) + (_dot(s1, w3) + _dot(s2, w2) + _dot(s3, w1))
    o_ref[...] = acc + b_ref[...]
    a = jnp.sum(q1_ref[...] * k1_ref[...], axis=-1, keepdims=True)
    b = jnp.sum(q2_ref[...] * k2_ref[...], axis=-1, keepdims=True)
    lam_ref[...] = jnp.broadcast_to(jnp.exp(a) - jnp.exp(b) + lambda_init, lam_ref.shape)


def _ada(c_all, w_ada, b_ada, lq1, lk1, lq2, lk2, lambda_init):
    n, d = c_all.shape
    n_out = w_ada.shape[1]
    tn = 512
    vec = pl.BlockSpec((1, DA), lambda j: (0, 0))
    return pl.pallas_call(
        functools.partial(_ada_kernel, lambda_init=lambda_init),
        out_shape=(jax.ShapeDtypeStruct((n, n_out), F32), jax.ShapeDtypeStruct((1, LANES), F32)),
        grid=(n_out // tn,),
        in_specs=[pl.BlockSpec((n, d), lambda j: (0, 0)), pl.BlockSpec((d, tn), lambda j: (0, j)),
                  pl.BlockSpec((1, tn), lambda j: (0, j)), vec, vec, vec, vec],
        out_specs=(pl.BlockSpec((n, tn), lambda j: (0, j)), pl.BlockSpec((1, LANES), lambda j: (0, 0))),
        compiler_params=_cparams(("arbitrary",)),
        name="ada",
    )(c_all, w_ada, b_ada.reshape(1, -1), lq1.reshape(1, -1), lk1.reshape(1, -1), lq2.reshape(1, -1), lk2.reshape(1, -1))


_W_QA, _W_KA, _W_VA, _W_QB = 0, 512, 1024, 1536
_W_KV6 = _W_QB + HB * LANES
_W_GB = _W_KV6 + 6 * 128
_W_GM = _W_GB + 128
_W_COLS = _W_GM + 2048


def _group_norm(p, gain, ones):
    n = p.shape[1]
    w = 256 if n % 256 == 0 else 128
    sq = (p * p).astype(BF16)
    parts = [_dot(sq[:, c:c + w], ones[:w, :w]) for c in range(0, n, w)]
    ss = parts[0] if len(parts) == 1 else jnp.concatenate(parts, axis=1)
    return p * lax.rsqrt(ss * (1.0 / 64.0) + RMS_EPS) * gain


def _proj_kernel(x_ref, sc_ref, sh_ref, ng_ref, w_ref, gqa_ref, gka_ref, gqb_ref, gks_ref, gkw_ref, ones_ref,
                 qa_ref, ka_ref, va_ref, qb_ref, kc_ref, vc_ref, ks_ref, vs_ref, kw_ref, vw_ref, gb_ref, gm_ref):
    x = x_ref[...]
    h = x * lax.rsqrt(jnp.mean(x * x, axis=-1, keepdims=True) + RMS_EPS) * ng_ref[...]
    h = h * (1.0 + sc_ref[...]) + sh_ref[...]
    hb = h.astype(BF16)
    ones = ones_ref[...]

    def seg(a, n):
        return _dot(hb, w_ref[:, a:a + n])

    qa_ref[...] = (_group_norm(seg(_W_QA, 512), gqa_ref[...], ones) * QK_SCALE).astype(BF16)
    ka_ref[...] = _group_norm(seg(_W_KA, 512), gka_ref[...], ones)
    va_ref[...] = seg(_W_VA, 512)
    qb_ref[...] = (_group_norm(seg(_W_QB, 1024), gqb_ref[...], ones) * QK_SCALE).astype(BF16)
    kc_ref[...] = seg(_W_KV6, 128)
    vc_ref[...] = seg(_W_KV6 + 128, 128)
    ks_ref[...] = _group_norm(seg(_W_KV6 + 256, 128), gks_ref[...], ones)
    vs_ref[...] = seg(_W_KV6 + 384, 128)
    kw_ref[...] = _group_norm(seg(_W_KV6 + 512, 128), gkw_ref[...], ones)
    vw_ref[...] = seg(_W_KV6 + 640, 128)
    gb_ref[...] = jax.nn.sigmoid(seg(_W_GB, 128))
    gm_ref[:, :1024] = jax.nn.sigmoid(seg(_W_GM, 1024)).astype(BF16)
    gm_ref[:, 1024:] = jax.nn.sigmoid(seg(_W_GM + 1024, 1024)).astype(BF16)


def _mod_spec(per_token, tm, bps, d):
    if per_token:
        return pl.BlockSpec((tm, d), lambda i, *_: (i, 0))
    return pl.BlockSpec((None, 1, d), lambda i, *_: (i // bps, 0, 0))


def _proj(x, sc, sh, per_token, rows_per_seq, wp):
    n, d = x.shape
    tm = min(ROW_TILE, n)
    bps = max(rows_per_seq // tm, 1)
    mod = _mod_spec(per_token, tm, bps, d)

    def row(width):
        return pl.BlockSpec((tm, width), lambda i: (i, 0))

    widths = [(512, BF16), (512, F32), (512, F32), (1024, BF16)] + [(128, F32)] * 6 + [(128, F32), (2048, BF16)]
    return pl.pallas_call(
        _proj_kernel,
        out_shape=tuple(jax.ShapeDtypeStruct((n, w), dt) for w, dt in widths),
        grid=(n // tm,),
        in_specs=[row(d), mod, mod, _const_spec((1, d)), _const_spec((d, _W_COLS)),
                  _const_spec((1, 512)), _const_spec((1, 512)), _const_spec((1, 1024)),
                  _const_spec((1, 128)), _const_spec((1, 128)), _const_spec((256, 256))],
        out_specs=tuple(row(w) for w, _ in widths),
        compiler_params=_cparams(("parallel",)),
        name="proj",
    )(x, sc, sh, wp["norm1_g"], wp["w_in"], wp["g_qa"], wp["g_ka"], wp["g_qb"], wp["g_ks"], wp["g_kw"], wp["ones"])


POS_RADIX = 256


def _pos_lanes(n_keys):
    k = jnp.arange(n_keys)
    return jnp.stack([k // POS_RADIX, k % POS_RADIX], axis=1).astype(F32)


def _flash_step(m, acc, s, valid, v_ext):
    if valid is not None:
        s = jnp.where(valid, s, NEG_INF)
    m_new = jnp.maximum(m, jnp.max(s, axis=-1, keepdims=True))
    alpha = jnp.exp(m - m_new)
    p = jnp.exp(s - m_new)
    acc = alpha * acc + _dot(p.astype(BF16), v_ext)
    return m_new, acc


DIFF_HEADS_PER_STEP = 4


def _diff_prompt_kernel(slope_ref, lam_ref, q_ref, k_ref, v_ref, pos_ref, hn_ref, o_ref, ka0, ka1, vext,
                        *, tq, out_scale):
    hp = pl.program_id(1)
    i = pl.program_id(2)
    lam = lam_ref[0]
    nh = DIFF_HEADS_PER_STEP

    @pl.when(i == 0)
    def _():
        pos = pos_ref[...]
        lane = lax.broadcasted_iota(jnp.int32, pos.shape, 1)
        for hh in range(nh):
            k = k_ref[:, hh * 2 * DA:(hh + 1) * 2 * DA]
            ka0[hh] = jnp.where(lane < DA, k, pos).astype(BF16)
            ka1[hh] = jnp.where(lane >= DA, k, pos).astype(BF16)
            vext[hh, :, :DVA] = v_ref[:, hh * DVA:(hh + 1) * DVA].astype(BF16)
            vext[hh, :, DVA:] = jnp.ones((vext.shape[1], DVA), BF16)

    lane = lax.broadcasted_iota(jnp.int32, (tq, 2 * DA), 1)
    half = lane % DA
    chains = []
    for hh in range(nh):
        slope = slope_ref[hp * nh + hh]
        q = q_ref[:, hh * 2 * DA:(hh + 1) * 2 * DA].astype(F32)
        digits = jnp.where(half == 0, slope * POS_RADIX, jnp.where(half == 1, slope, 0.0))
        chains.append((hh, ka0, jnp.where(lane < DA, q, digits).astype(BF16)))
        chains.append((hh, ka1, jnp.where(lane >= DA, q, digits).astype(BF16)))
    r = lax.broadcasted_iota(jnp.int32, (tq, tq), 0)
    c = lax.broadcasted_iota(jnp.int32, (tq, tq), 1)
    causal = r >= c

    def block(j, carry, valid):
        start = pl.multiple_of(j * tq, tq)
        out = []
        for (hh, kaug, qm), (m, acc) in zip(chains, carry):
            s = _dot_nt(qm, kaug[hh, pl.ds(start, tq), :])
            out.append(_flash_step(m, acc, s, valid, vext[hh, pl.ds(start, tq), :]))
        return tuple(out)

    init = tuple((jnp.full((tq, 1), NEG_INF, F32), jnp.zeros((tq, 2 * DVA), F32)) for _ in chains)
    carry = lax.fori_loop(0, i, lambda j, cr: block(j, cr, None), init)
    carry = block(i, carry, causal)
    outs = [acc[:, :DVA] / acc[:, DVA:] for _, acc in carry]
    for hh in range(nh):
        o = outs[2 * hh] - lam * outs[2 * hh + 1]
        o = o * lax.rsqrt(jnp.mean(o * o, axis=-1, keepdims=True) + RMS_EPS) * hn_ref[...]
        o_ref[:, hh * DVA:(hh + 1) * DVA] = (o * out_scale).astype(BF16)


def _diff_prompt(qa, ka, va, slopes, lam, hn, out_scale):
    b, t, _ = qa.shape
    tq = min(DIFF_TQ, t)
    nh = DIFF_HEADS_PER_STEP
    assert t <= POS_RADIX * POS_RADIX and HA % nh == 0
    smem = pl.BlockSpec(memory_space=pltpu.SMEM)
    pos = jnp.pad(_pos_lanes(t), ((0, 0), (0, DA - 2)))
    pos = jnp.concatenate([pos, pos], axis=1)
    return pl.pallas_call(
        functools.partial(_diff_prompt_kernel, tq=tq, out_scale=out_scale),
        out_shape=jax.ShapeDtypeStruct((b, t, HA * DVA), BF16),
        grid=(b, HA // nh, t // tq),
        in_specs=[smem, smem,
                  pl.BlockSpec((None, tq, nh * 2 * DA), lambda bi, h, i: (bi, i, h)),
                  pl.BlockSpec((None, t, nh * 2 * DA), lambda bi, h, i: (bi, 0, h)),
                  pl.BlockSpec((None, t, nh * DVA), lambda bi, h, i: (bi, 0, h)),
                  pl.BlockSpec((t, 2 * DA), lambda bi, h, i: (0, 0)),
                  pl.BlockSpec((1, DVA), lambda bi, h, i: (0, 0))],
        out_specs=pl.BlockSpec((None, tq, nh * DVA), lambda bi, h, i: (bi, i, h)),
        scratch_shapes=[pltpu.VMEM((nh, t, 2 * DA), BF16), pltpu.VMEM((nh, t, 2 * DA), BF16),
                        pltpu.VMEM((nh, t, 2 * DVA), BF16)],
        compiler_params=_cparams(("parallel", "parallel", "arbitrary")),
        name="diff_prompt",
    )(slopes, lam, qa, ka, va, pos, hn)


def _compress(buf_ref, m_rows, w1_ref, pe_ref, b1_ref, w2_ref, b2_ref):
    acc = jnp.zeros((m_rows, KVH * CMP_HID), F32)
    group = 8
    for jg in range(L_CMP // group):
        cols = slice(jg * group * LANES, (jg + 1) * group * LANES)
        xs = [buf_ref[pl.ds(jg * group + jj, m_rows, stride=CMP_STRIDE), :] for jj in range(group)]
        x = jnp.concatenate(xs, axis=1) + pe_ref[:, cols]
        acc = acc + _dot(x.astype(BF16), w1_ref[cols, :])
    hid = _gelu_tanh(acc + b1_ref[...])
    return _dot(hid.astype(BF16), w2_ref[...]) + b2_ref[...]


def _cmp_attention(q_ext, kcmp_b, vcmp_b, qpos, slopes_h, gsum, nsb, n_q_rows):
    m_rows = kcmp_b.shape[0]
    cend = (lax.broadcasted_iota(jnp.int32, (1, m_rows), 1) * CMP_STRIDE + (L_CMP - 1))
    visible = cend <= qpos
    dist = (qpos - cend).astype(F32)
    outs = []
    psum = jnp.zeros((n_q_rows, m_rows), F32)
    for g in range(GB):
        s = _dot_nt(q_ext[g], kcmp_b) - slopes_h[g] * dist
        s = jnp.where(visible, s, NEG_INF)
        mx = jnp.max(s, axis=-1, keepdims=True)
        e = jnp.where(visible, jnp.exp(s - mx), 0.0)
        p = e / jnp.maximum(jnp.sum(e, axis=-1, keepdims=True), 1e-30)
        outs.append(_dot(p.astype(BF16), vcmp_b))
        psum = psum + p
    return outs, _select_blocks(psum, qpos, gsum, nsb)


def _select_blocks(psum, qpos, gsum, nsb):
    p1, p2, p3 = _split3(psum)
    imp = _dot(p1, gsum) + _dot(p2, gsum) + _dot(p3, gsum)
    blk = lax.broadcasted_iota(jnp.int32, (1, imp.shape[1]), 1)
    cur = qpos // SEL_BLOCK
    forced = (blk == 0) | (blk == cur) | (blk == cur - 1)
    imp = jnp.where(blk > cur, -BIG, imp)
    imp = jnp.where(forced, BIG, imp)
    imp = jnp.where(blk >= nsb, REMOVED, imp)
    return _topk_mask(imp, min(N_SEL, nsb))


def _cmp_prompt_kernel(q_ref, kc_ref, vc_ref, w1k_ref, pek_ref, b1k_ref, w2k_ref, b2k_ref,
                       w1v_ref, pev_ref, b1v_ref, w2v_ref, b2v_ref, kn_ref, ones_ref, gsum_ref,
                       o_ref, sel_ref, bufk, bufv, kcmp, vcmp, *, t, tq, m_rows, nsb):
    i = pl.program_id(1)

    @pl.when(i == 0)
    def _():
        for buf, src in ((bufk, kc_ref), (bufv, vc_ref)):
            buf[pl.ds(0, t), :] = src[...]
            buf[pl.ds(t, buf.shape[0] - t), :] = jnp.zeros((buf.shape[0] - t, LANES), F32)
        k = _compress(bufk, m_rows, w1k_ref, pek_ref, b1k_ref, w2k_ref, b2k_ref)
        kcmp[...] = _group_norm(k, kn_ref[...], ones_ref[...]).astype(BF16)
        vcmp[...] = _compress(bufv, m_rows, w1v_ref, pev_ref, b1v_ref, w2v_ref, b2v_ref).astype(BF16)

    qpos = i * tq + lax.broadcasted_iota(jnp.int32, (tq, 1), 0)
    kcmp_b = kcmp[...]
    vcmp_b = vcmp[...]
    gsum = gsum_ref[...]
    for kvh in range(KVH):
        q_ext = [q_ref[:, (kvh * GB + g) * LANES:(kvh * GB + g + 1) * LANES] for g in range(GB)]
        outs, sel = _cmp_attention(q_ext, kcmp_b, vcmp_b, qpos, SLOPES_B[kvh * GB:(kvh + 1) * GB], gsum, nsb, tq)
        for g in range(GB):
            o_ref[:, (kvh * GB + g) * LANES:(kvh * GB + g + 1) * LANES] = outs[g].astype(BF16)
        blk = lax.broadcasted_iota(jnp.int32, sel.shape, 1)
        neg = jnp.where(blk < nsb, (sel - 1.0) * MASK_NEG, 0.0)
        if kvh == 0:
            neg = pltpu.roll(neg, DB, 1)
        sel_ref[kvh] = neg.astype(BF16)


def _cmp_prompt(qb_ext, kc, vc, wp):
    b, t, _ = qb_ext.shape
    tq = min(CMP_TQ, t)
    nc = t // CMP_STRIDE
    m_rows = _round_up(nc, LANES)
    nsb = t // SEL_BLOCK
    nsbp = _round_up(nsb, LANES)
    assert nsbp == LANES and nsb <= AUG_POS
    buf_rows = CMP_STRIDE * (m_rows - 1) + L_CMP
    gsum = ((jnp.arange(m_rows)[:, None] // (SEL_BLOCK // CMP_STRIDE) == jnp.arange(nsbp)[None, :])
            & (jnp.arange(m_rows)[:, None] < nc)).astype(BF16)
    kv = pl.BlockSpec((None, t, LANES), lambda bi, i: (bi, 0, 0))
    cw = [_const_spec((L_CMP * LANES, KVH * CMP_HID)), _const_spec((1, L_CMP * LANES)), _const_spec((1, KVH * CMP_HID)),
          _const_spec((KVH * CMP_HID, LANES)), _const_spec((1, LANES))]
    return pl.pallas_call(
        functools.partial(_cmp_prompt_kernel, t=t, tq=tq, m_rows=m_rows, nsb=nsb),
        out_shape=(jax.ShapeDtypeStruct((b, t, HB * LANES), BF16), jax.ShapeDtypeStruct((b, KVH, t, nsbp), BF16)),
        grid=(b, t // tq),
        in_specs=[pl.BlockSpec((None, tq, HB * LANES), lambda bi, i: (bi, i, 0)), kv, kv] + cw + cw
                 + [_const_spec((1, LANES)), _const_spec((256, 256)), _const_spec((m_rows, nsbp))],
        out_specs=(pl.BlockSpec((None, tq, HB * LANES), lambda bi, i: (bi, i, 0)),
                   pl.BlockSpec((None, KVH, tq, nsbp), lambda bi, i: (bi, 0, i, 0))),
        scratch_shapes=[pltpu.VMEM((buf_rows, LANES), F32), pltpu.VMEM((buf_rows, LANES), F32),
                        pltpu.VMEM((m_rows, LANES), BF16), pltpu.VMEM((m_rows, LANES), BF16)],
        compiler_params=_cparams(("parallel", "arbitrary")),
        name="cmp_prompt",
    )(qb_ext, kc, vc, *wp["cmp_k"], *wp["cmp_v"], wp["g_kc"], wp["ones"], gsum)


def _compact_pairs(ext, kvh):
    lane = lax.broadcasted_iota(jnp.int32, ext[0].shape, 1)
    pieces = []
    for pair in range(GB // 2):
        even, odd = ext[2 * pair], ext[2 * pair + 1]
        if kvh == 0:
            pieces.append(jnp.where(lane < DB, even, pltpu.roll(odd, DB, 1)))
        else:
            pieces.append(jnp.where(lane < DB, pltpu.roll(even, DB, 1), odd))
    return jnp.concatenate(pieces, axis=1)


def _nsa_prompt_kernel(q_ref, ks_ref, vs_ref, kw_ref, vw_ref, sel_ref, ocmp_ref, gb_ref, aug_ref, o_ref,
                       ka_s, va_s, ka_w, va_w, *, tq, tk):
    i = pl.program_id(1)
    rows = 2 * tq

    @pl.when(i == 0)
    def _():
        lane = lax.broadcasted_iota(jnp.int32, ks_ref.shape, 1)
        for kvh in range(KVH):
            own = (lane >= kvh * DB) & (lane < (kvh + 1) * DB)
            aug = aug_ref[kvh]
            ka_s[kvh] = jnp.where(own, ks_ref[...], aug).astype(BF16)
            ka_w[kvh] = jnp.where(own, kw_ref[...], aug).astype(BF16)
            va_s[kvh] = jnp.where(own, vs_ref[...], 1.0).astype(BF16)
            va_w[kvh] = jnp.where(own, vw_ref[...], 1.0).astype(BF16)

    r = lax.broadcasted_iota(jnp.int32, (rows, tk), 0) % tq
    c = lax.broadcasted_iota(jnp.int32, (rows, tk), 1)
    dbase = r - c
    qlane = lax.broadcasted_iota(jnp.int32, (tq, LANES), 1)
    gates = gb_ref[...]
    j_hi = (i * tq + tq - 1) // tk
    j_lo_win = jnp.maximum(i * tq - (WINDOW - 1), 0) // tk

    chains = []
    for kvh in range(KVH):
        base = (1 - kvh) * DB
        own_q = (qlane >= kvh * DB) & (qlane < (kvh + 1) * DB)
        negmask = sel_ref[kvh].astype(F32)
        q_sel, q_win = [], []
        for g in range(GB):
            head = kvh * GB + g
            slot = q_ref[:, head * LANES:(head + 1) * LANES].astype(F32)
            digits = jnp.where(qlane == base + AUG_POS, SLOPES_B[head] * POS_RADIX,
                               jnp.where(qlane == base + AUG_POS + 1, SLOPES_B[head], 0.0))
            q_win.append(jnp.where(own_q, slot, digits).astype(BF16))
            q_sel.append(jnp.where(own_q, slot, digits + negmask).astype(BF16))
        for g in range(0, GB, 2):
            chains.append((kvh, g, jnp.concatenate(q_sel[g:g + 2], axis=0), jnp.concatenate(q_win[g:g + 2], axis=0)))

    def sel_block(j, carry, valid=None):
        start = pl.multiple_of(j * tk, tk)
        out = []
        for (kvh, _, qs, _), (m, acc) in zip(chains, carry):
            s = _dot_nt(qs, ka_s[kvh, pl.ds(start, tk), :])
            out.append(_flash_step(m, acc, s, valid, va_s[kvh, pl.ds(start, tk), :]))
        return tuple(out)

    def win_block(j, carry):
        start = pl.multiple_of(j * tk, tk)
        dist = dbase - (j * tk - i * tq)
        valid = (dist >= 0) & (dist < WINDOW)
        out = []
        for (kvh, _, _, qw), (m, acc) in zip(chains, carry):
            s = _dot_nt(qw, ka_w[kvh, pl.ds(start, tk), :])
            out.append(_flash_step(m, acc, s, valid, va_w[kvh, pl.ds(start, tk), :]))
        return tuple(out)

    init = tuple((jnp.full((rows, 1), NEG_INF, F32), jnp.zeros((rows, LANES), F32)) for _ in chains)
    carry = lax.fori_loop(0, j_hi, sel_block, init)
    res_s = sel_block(j_hi, carry, dbase >= j_hi * tk - i * tq)
    res_w = lax.fori_loop(j_lo_win, j_hi + 1, win_block, init)

    def finish(acc):
        den = pltpu.roll(acc, DB, 1)
        return acc / jnp.maximum(den, 1e-30)

    ext = [[None] * GB for _ in range(KVH)]
    for (kvh, g0, _, _), (_, acc_s), (_, acc_w) in zip(chains, res_s, res_w):
        o_s, o_w = finish(acc_s), finish(acc_w)
        for gi in range(2):
            g = g0 + gi
            head = kvh * GB + g
            col = head * 3
            oc = ocmp_ref[:, head * LANES:(head + 1) * LANES].astype(F32)
            ext[kvh][g] = (gates[:, col:col + 1] * oc + gates[:, col + 1:col + 2] * o_s[gi * tq:(gi + 1) * tq]
                           + gates[:, col + 2:col + 3] * o_w[gi * tq:(gi + 1) * tq])
    for kvh in range(KVH):
        o_ref[:, kvh * GB * DB:(kvh + 1) * GB * DB] = _compact_pairs(ext[kvh], kvh).astype(BF16)


def _nsa_prompt(qb_ext, ks, vs, kw, vw, sel, ocmp, gbs):
    b, t, _ = qb_ext.shape
    tq = min(NSA_TQ, t)
    tk = min(NSA_TK, t)
    nsb = t // SEL_BLOCK
    assert tk % tq == 0 and nsb <= AUG_POS and t <= POS_RADIX * POS_RADIX
    half = jnp.concatenate([(jnp.arange(t)[:, None] // SEL_BLOCK == jnp.arange(AUG_POS)[None, :]).astype(F32),
                            _pos_lanes(t)], axis=1)
    zeros = jnp.zeros((t, DB), F32)
    aug = jnp.stack([jnp.concatenate([zeros, half], axis=1), jnp.concatenate([half, zeros], axis=1)])
    kv = pl.BlockSpec((None, t, LANES), lambda bi, i: (bi, 0, 0))
    scratch = pltpu.VMEM((KVH, t, LANES), BF16)
    return pl.pallas_call(
        functools.partial(_nsa_prompt_kernel, tq=tq, tk=tk),
        out_shape=jax.ShapeDtypeStruct((b, t, HB * DB), BF16),
        grid=(b, t // tq),
        in_specs=[pl.BlockSpec((None, tq, HB * LANES), lambda bi, i: (bi, i, 0)), kv, kv, kv, kv,
                  pl.BlockSpec((None, KVH, tq, LANES), lambda bi, i: (bi, 0, i, 0)),
                  pl.BlockSpec((None, tq, HB * LANES), lambda bi, i: (bi, i, 0)),
                  pl.BlockSpec((None, tq, LANES), lambda bi, i: (bi, i, 0)),
                  _const_spec((KVH, t, LANES))],
        out_specs=pl.BlockSpec((None, tq, HB * DB), lambda bi, i: (bi, i, 0)),
        scratch_shapes=[scratch, scratch, scratch, scratch],
        compiler_params=_cparams(("parallel", "arbitrary")),
        name="nsa_prompt",
    )(qb_ext, ks, vs, kw, vw, sel, ocmp, gbs, aug)


def _merge_kernel(oa_ref, ob_ref, gm_ref, x_ref, g1_ref, sc_ref, sh_ref, ng_ref, wa_ref, wb_ref, wo_ref,
                  wrh_ref, wrl_ref, br_ref, x1_ref, h2_ref, cw_ref):
    ya = _dot(oa_ref[...], wa_ref[...])
    yb = _dot(ob_ref[...], wb_ref[...])
    d = ya.shape[1]
    t = gm_ref[:, :d].astype(F32) * ya + gm_ref[:, d:].astype(F32) * yb
    mix = _dot(t.astype(BF16), wo_ref[...])
    x1 = x_ref[...] + g1_ref[...] * mix
    x1_ref[...] = x1
    h2 = x1 * lax.rsqrt(jnp.mean(x1 * x1, axis=-1, keepdims=True) + RMS_EPS) * ng_ref[...]
    h2 = h2 * (1.0 + sc_ref[...]) + sh_ref[...]
    h2_ref[...] = h2.astype(BF16)
    hh = h2.astype(BF16)
    hl = (h2 - hh.astype(F32)).astype(BF16)
    logits = _dot(hh, wrh_ref[...]) + (_dot(hh, wrl_ref[...]) + _dot(hl, wrh_ref[...])) + br_ref[...]
    lane = lax.broadcasted_iota(jnp.int32, logits.shape, 1)
    lane_f = lane.astype(F32)
    is_grp = lane < N_GROUPS
    lg = jnp.where(is_grp, logits, NEG_INF)
    gmax = jnp.max(lg, axis=-1, keepdims=True)
    gidx = jnp.min(jnp.where(lg == gmax, lane_f, float(LANES)), axis=-1, keepdims=True)
    p_grp = 1.0 / jnp.sum(jnp.where(is_grp, jnp.exp(lg - gmax), 0.0), axis=-1, keepdims=True)
    egrp = ((lane - N_GROUPS) // EXPERTS_PER_GROUP).astype(F32)
    in_grp = (lane >= N_GROUPS) & (lane < N_GROUPS + N_EXPERTS) & (egrp == gidx)
    le = jnp.where(in_grp, logits, REMOVED)
    v1 = jnp.max(le, axis=-1, keepdims=True)
    i1 = jnp.min(jnp.where(le == v1, lane_f, float(LANES)), axis=-1, keepdims=True)
    le2 = jnp.where(lane_f == i1, REMOVED, le)
    v2 = jnp.max(le2, axis=-1, keepdims=True)
    i2 = jnp.min(jnp.where(le2 == v2, lane_f, float(LANES)), axis=-1, keepdims=True)
    e2 = jnp.exp(v2 - v1)
    w1 = 1.0 / (1.0 + e2)
    cw_ref[...] = jnp.where(lane_f == i1, w1 * p_grp, jnp.where(lane_f == i2, e2 * w1 * p_grp, 0.0))


def _merge(oa, ob, gm, x, g1, sc2, sh2, per_token, rows_per_seq, wp):
    n, d = x.shape
    tm = min(ROW_TILE, n)
    bps = max(rows_per_seq // tm, 1)
    mod = _mod_spec(per_token, tm, bps, d)

    def row(width):
        return pl.BlockSpec((tm, width), lambda i: (i, 0))

    return pl.pallas_call(
        _merge_kernel,
        out_shape=(jax.ShapeDtypeStruct((n, d), F32), jax.ShapeDtypeStruct((n, d), BF16),
                   jax.ShapeDtypeStruct((n, LANES), F32)),
        grid=(n // tm,),
        in_specs=[row(HA * DVA), row(HB * DB), row(2 * d), row(d), mod, mod, mod, _const_spec((1, d)),
                  _const_spec((HA * DVA, d)), _const_spec((HB * DB, d)), _const_spec((d, d)),
                  _const_spec((d, LANES)), _const_spec((d, LANES)), _const_spec((1, LANES))],
        out_specs=(row(d), row(d), row(LANES)),
        compiler_params=_cparams(("parallel",)),
        name="merge",
    )(oa, ob, gm, x, g1, sc2, sh2, wp["norm2_g"], wp["w_up_a"], wp["w_up_b"], wp["w_out"],
      wp["w_r_hi"], wp["w_r_lo"], wp["b_r"])


def _moe_kernel(h_ref, cw_ref, x_ref, g2_ref, wg_ref, wu_ref, wd_ref, o_ref, acc_ref):
    grp = pl.program_id(1)

    @pl.when(grp == 0)
    def _():
        acc_ref[...] = jnp.zeros_like(acc_ref)

    lane = lax.broadcasted_iota(jnp.int32, cw_ref.shape, 1)
    cw_all = cw_ref[...]
    h = h_ref[...]
    a = _dot(h, wg_ref[...])
    u = _dot(h, wu_ref[...])
    cols = []
    for el in range(EXPERTS_PER_GROUP):
        cw = jnp.sum(jnp.where(lane == N_GROUPS + grp * EXPERTS_PER_GROUP + el, cw_all, 0.0), axis=-1, keepdims=True)
        ae = a[:, el * D_EXPERT:(el + 1) * D_EXPERT]
        cols.append((ae * jax.nn.sigmoid(ae) * u[:, el * D_EXPERT:(el + 1) * D_EXPERT] * cw).astype(BF16))
    acc_ref[...] += _dot(jnp.concatenate(cols, axis=1), wd_ref[...])

    @pl.when(grp == pl.num_programs(1) - 1)
    def _():
        o_ref[...] = x_ref[...] + g2_ref[...] * acc_ref[...]


def _moe(h2, cw, x1, g2, per_token, rows_per_seq, wp):
    n, d = x1.shape
    tm = min(MOE_ROW_TILE, n)
    bps = max(rows_per_seq // tm, 1)
    gw = EXPERTS_PER_GROUP * D_EXPERT
    if per_token:
        mod = pl.BlockSpec((tm, d), lambda i, e: (i, 0))
    else:
        mod = pl.BlockSpec((None, 1, d), lambda i, e: (i // bps, 0, 0))
    return pl.pallas_call(
        _moe_kernel,
        out_shape=jax.ShapeDtypeStruct((n, d), F32),
        grid=(n // tm, N_GROUPS),
        in_specs=[pl.BlockSpec((tm, d), lambda i, e: (i, 0)), pl.BlockSpec((tm, LANES), lambda i, e: (i, 0)),
                  pl.BlockSpec((tm, d), lambda i, e: (i, 0)), mod,
                  pl.BlockSpec((None, d, gw), lambda i, e: (e, 0, 0)),
                  pl.BlockSpec((None, d, gw), lambda i, e: (e, 0, 0)),
                  pl.BlockSpec((None, gw, d), lambda i, e: (e, 0, 0))],
        out_specs=pl.BlockSpec((tm, d), lambda i, e: (i, 0)),
        scratch_shapes=[pltpu.VMEM((tm, d), F32)],
        compiler_params=_cparams(("parallel", "arbitrary")),
        name="moe",
    )(h2, cw, x1, g2, wp["w_gate"], wp["w_up"], wp["w_down"])


def _page_specs(n_pages, ppc, rows, width):
    def spec(p):
        return pl.BlockSpec((None, rows, width), lambda s, c, pt: (pt[s * n_pages + c * ppc + p], 0, 0))
    return [spec(p) for p in range(ppc)]


def _pad_rows(x, rows):
    return jnp.concatenate([x, jnp.zeros((rows - x.shape[0], x.shape[1]), x.dtype)], axis=0)


def _online_update(m_ref, l_ref, acc_ref, s, valid, pv_fn):
    if valid is not None:
        s = jnp.where(valid, s, NEG_INF)
    m_old = m_ref[...]
    m_new = jnp.maximum(m_old, jnp.max(s, axis=-1, keepdims=True))
    alpha = jnp.exp(m_old - m_new)
    p = jnp.exp(s - m_new)
    if valid is not None:
        p = jnp.where(valid, p, 0.0)
    m_ref[...] = m_new
    l_ref[...] = alpha * l_ref[...] + jnp.sum(p, axis=-1, keepdims=True)
    acc_ref[...] = alpha * acc_ref[...] + pv_fn(p.astype(BF16))


def _diff_sample_kernel(pt_ref, lam_ref, q_ref, kn_ref, vn_ref, hn_ref, *rest, ppc, page, past_len, n_new, out_scale):
    k_refs, v_refs = rest[:ppc], rest[ppc:2 * ppc]
    o_ref, m_ref, l_ref, acc_ref = rest[2 * ppc:]
    c = pl.program_id(1)
    rows = HA * 2 * 8
    hrows = 2 * 8
    lam = lam_ref[0]

    @pl.when(c == 0)
    def _():
        m_ref[...] = jnp.full(m_ref.shape, NEG_INF, F32)
        l_ref[...] = jnp.zeros_like(l_ref)
        acc_ref[...] = jnp.zeros_like(acc_ref)

    ridx = lax.broadcasted_iota(jnp.int32, (rows, 1), 0)
    qpos = past_len + ridx % 8
    slope = jnp.zeros((rows, 1), F32)
    for h in range(HA):
        slope = jnp.where(ridx // hrows == h, SLOPES_A[h], slope)
    q = q_ref[...]

    def per_head(p, v_heads):
        return jnp.concatenate([_dot(p[h * hrows:(h + 1) * hrows], v_heads[h]) for h in range(HA)], axis=0)

    chunk = ppc * page
    kt = jnp.concatenate([r[...] for r in k_refs], axis=1).astype(BF16)
    v_heads = [jnp.concatenate([r[pl.ds(h, page, stride=HA), :] for r in v_refs], axis=0).astype(BF16)
               for h in range(HA)]
    kpos = c * chunk + lax.broadcasted_iota(jnp.int32, (1, chunk), 1)
    s = _dot(q, kt) - slope * (qpos - kpos).astype(F32)
    _online_update(m_ref, l_ref, acc_ref, s, None, lambda p: per_head(p, v_heads))

    @pl.when(c == pl.num_programs(1) - 1)
    def _():
        kn = _pad_rows(kn_ref[...], LANES).astype(BF16)
        vn = _pad_rows(vn_ref[...], LANES).astype(BF16)
        vn_heads = [vn[:, h * DVA:(h + 1) * DVA] for h in range(HA)]
        npos = past_len + lax.broadcasted_iota(jnp.int32, (1, LANES), 1)
        s_new = _dot_nt(q, kn) - slope * (qpos - npos).astype(F32)
        _online_update(m_ref, l_ref, acc_ref, s_new, (npos <= qpos) & (npos < past_len + n_new),
                       lambda p: per_head(p, vn_heads))
        o_all = acc_ref[...] / l_ref[...]
        for h in range(HA):
            r0 = h * hrows
            o = o_all[r0:r0 + 8] - lam * o_all[r0 + 8:r0 + 16]
            o = o * lax.rsqrt(jnp.mean(o * o, axis=-1, keepdims=True) + RMS_EPS) * hn_ref[...]
            o_ref[:, h * DVA:(h + 1) * DVA] = (o * out_scale).astype(BF16)


def _diff_sample(pt, lam, qbd, k_new, v_new, hn, pool_kt, pool_v, n_pages, past_len, n_new, out_scale):
    s = qbd.shape[0]
    page = pool_kt.shape[2]
    ppc = math.gcd(DIFF_PAGES_PER_STEP, n_pages)
    rows = HA * 2 * 8
    per_seq = lambda shape: pl.BlockSpec((None,) + shape, lambda si, c, pt: (si, 0, 0))
    return pl.pallas_call(
        functools.partial(_diff_sample_kernel, ppc=ppc, page=page, past_len=past_len, n_new=n_new, out_scale=out_scale),
        out_shape=jax.ShapeDtypeStruct((s, 8, HA * DVA), BF16),
        grid_spec=pltpu.PrefetchScalarGridSpec(
            num_scalar_prefetch=1, grid=(s, n_pages // ppc),
            in_specs=[pl.BlockSpec(memory_space=pltpu.SMEM), per_seq((rows, HA * 2 * DA)),
                      per_seq((8, HA * 2 * DA)), per_seq((8, HA * DVA)), _const_spec((1, DVA))]
                     + _page_specs(n_pages, ppc, HA * 2 * DA, page) + _page_specs(n_pages, ppc, page * HA, DVA),
            out_specs=per_seq((8, HA * DVA)),
            scratch_shapes=[pltpu.VMEM((rows, 1), F32), pltpu.VMEM((rows, 1), F32), pltpu.VMEM((rows, DVA), F32)]),
        compiler_params=_cparams(("parallel", "arbitrary")),
        name="diff_sample",
    )(pt, lam, qbd, k_new, v_new, hn, *([pool_kt] * ppc), *([pool_v] * ppc))


def _cmp_sample_kernel(pt_ref, q_ref, kn_ref, vn_ref, w1k_ref, pek_ref, b1k_ref, w2k_ref, b2k_ref,
                       w1v_ref, pev_ref, b1v_ref, w2v_ref, b2v_ref, kng_ref, ones_ref, gsum_ref, *rest,
                       ppc, page, past_len, m_rows, mc_rows, nsb):
    k_refs, v_refs = rest[:ppc], rest[ppc:2 * ppc]
    o_ref, sel_ref, bufk, bufv = rest[2 * ppc:]
    c = pl.program_id(1)

    @pl.when(c == 0)
    def _():
        tail = bufk.shape[0] - past_len
        for buf, new in ((bufk, kn_ref), (bufv, vn_ref)):
            buf[pl.ds(past_len, tail), :] = jnp.zeros((tail, LANES), F32)
            buf[pl.ds(past_len, 8), :] = new[...]

    base = c * (ppc * page)
    for p in range(ppc):
        start = pl.multiple_of(base + p * page, page)
        bufk[pl.ds(start, page), :] = k_refs[p][...]
        bufv[pl.ds(start, page), :] = v_refs[p][...]

    @pl.when(c == pl.num_programs(1) - 1)
    def _():
        pad = jnp.zeros((m_rows - mc_rows, LANES), F32)
        k = _compress(bufk, mc_rows, w1k_ref, pek_ref, b1k_ref, w2k_ref, b2k_ref)
        k = _group_norm(k, kng_ref[...], ones_ref[...])
        kcmp_b = jnp.concatenate([k, pad], axis=0).astype(BF16)
        v = _compress(bufv, mc_rows, w1v_ref, pev_ref, b1v_ref, w2v_ref, b2v_ref)
        vcmp_b = jnp.concatenate([v, pad], axis=0).astype(BF16)
        rows = HB * 8
        ridx = lax.broadcasted_iota(jnp.int32, (rows, 1), 0)
        qpos = past_len + ridx % 8
        slope = jnp.zeros((rows, 1), F32)
        for h in range(HB):
            slope = jnp.where(ridx // 8 == h, SLOPES_B[h], slope)
        cend = lax.broadcasted_iota(jnp.int32, (1, m_rows), 1) * CMP_STRIDE + (L_CMP - 1)
        visible = cend <= qpos
        s = _dot_nt(q_ref[...], kcmp_b) - slope * (qpos - cend).astype(F32)
        s = jnp.where(visible, s, NEG_INF)
        e = jnp.where(visible, jnp.exp(s - jnp.max(s, axis=-1, keepdims=True)), 0.0)
        p = e / jnp.maximum(jnp.sum(e, axis=-1, keepdims=True), 1e-30)
        o_ref[...] = _dot(p.astype(BF16), vcmp_b)
        gsum = gsum_ref[...]
        for kvh in range(KVH):
            psum = p[kvh * GB * 8:kvh * GB * 8 + 8]
            for g in range(1, GB):
                psum = psum + p[(kvh * GB + g) * 8:(kvh * GB + g + 1) * 8]
            sel_ref[kvh] = _select_blocks(psum, qpos[:8], gsum, nsb)


def _cmp_sample(pt, q_heads, k_new, v_new, pool_k, pool_v, n_pages, past_len, n_new, wp):
    s = q_heads.shape[0]
    page = pool_k.shape[1]
    ppc = math.gcd(NSA_PAGES_PER_STEP, n_pages)
    tk_pad = _round_up(past_len + n_new, SEL_BLOCK)
    nc = tk_pad // CMP_STRIDE
    nsb = tk_pad // SEL_BLOCK
    m_rows = _round_up(nc, LANES)
    nsbp = _round_up(nsb, LANES)
    mc_rows = _round_up(nc, 8)
    buf_rows = CMP_STRIDE * (mc_rows - 1) + L_CMP
    gsum = ((jnp.arange(m_rows)[:, None] // (SEL_BLOCK // CMP_STRIDE) == jnp.arange(nsbp)[None, :])
            & (jnp.arange(m_rows)[:, None] < nc)).astype(BF16)
    cw = [_const_spec((L_CMP * LANES, KVH * CMP_HID)), _const_spec((1, L_CMP * LANES)), _const_spec((1, KVH * CMP_HID)),
          _const_spec((KVH * CMP_HID, LANES)), _const_spec((1, LANES))]
    per_seq = lambda shape: pl.BlockSpec((None,) + shape, lambda si, c, pt: (si,) + (0,) * len(shape))
    return pl.pallas_call(
        functools.partial(_cmp_sample_kernel, ppc=ppc, page=page, past_len=past_len, m_rows=m_rows, mc_rows=mc_rows,
                          nsb=nsb),
        out_shape=(jax.ShapeDtypeStruct((s, HB * 8, LANES), F32), jax.ShapeDtypeStruct((s, KVH, 8, nsbp), F32)),
        grid_spec=pltpu.PrefetchScalarGridSpec(
            num_scalar_prefetch=1, grid=(s, n_pages // ppc),
            in_specs=[per_seq((HB * 8, LANES)), per_seq((8, LANES)), per_seq((8, LANES))] + cw + cw
                     + [_const_spec((1, LANES)), _const_spec((256, 256)), _const_spec((m_rows, nsbp))]
                     + _page_specs(n_pages, ppc, page, LANES) + _page_specs(n_pages, ppc, page, LANES),
            out_specs=(per_seq((HB * 8, LANES)), per_seq((KVH, 8, nsbp))),
            scratch_shapes=[pltpu.VMEM((buf_rows, LANES), F32), pltpu.VMEM((buf_rows, LANES), F32)]),
        compiler_params=_cparams(("parallel", "arbitrary")),
        name="cmp_sample",
    )(pt, q_heads, k_new, v_new, *wp["cmp_k"], *wp["cmp_v"], wp["g_kc"], wp["ones"], gsum,
      *([pool_k] * ppc), *([pool_v] * ppc))


def _nsa_sample_kernel(pt_ref, q_ref, sel_ref, neg_ref, e_ref, kn_ref, vn_ref, kwin_ref, vwin_ref, kwn_ref, vwn_ref,
                       ocmp_ref, gate_ref, *rest, ppc, page, past_len, n_new, nsb, w_buf):
    k_refs, v_refs = rest[:ppc], rest[ppc:2 * ppc]
    o_ref, m_ref, l_ref, acc_ref, mw_ref, lw_ref, accw_ref = rest[2 * ppc:]
    c = pl.program_id(1)
    rows = HB * 8

    @pl.when(c == 0)
    def _():
        m_ref[...] = jnp.full(m_ref.shape, NEG_INF, F32)
        l_ref[...] = jnp.zeros_like(l_ref)
        acc_ref[...] = jnp.zeros_like(acc_ref)

    ridx = lax.broadcasted_iota(jnp.int32, (rows, 1), 0)
    qpos = past_len + ridx % 8
    slope = jnp.zeros((rows, 1), F32)
    for h in range(HB):
        slope = jnp.where(ridx // 8 == h, SLOPES_B[h], slope)
    q = q_ref[...]
    selm = jnp.concatenate([sel_ref[kvh] for kvh in range(KVH) for _ in range(GB)], axis=0)

    def bias(kpos):
        return slope * (qpos - kpos).astype(F32)

    chunk = ppc * page
    kt = jnp.concatenate([r[...] for r in k_refs], axis=1).astype(BF16)
    vt = jnp.concatenate([r[...] for r in v_refs], axis=1).astype(BF16)
    kpos = c * chunk + lax.broadcasted_iota(jnp.int32, (1, chunk), 1)
    q_aug = jnp.concatenate([q, neg_ref[...]], axis=1)
    kt_aug = jnp.concatenate([kt, e_ref[...]], axis=0)
    _online_update(m_ref, l_ref, acc_ref, _dot(q_aug, kt_aug) - bias(kpos), None, lambda p: _dot_nt(p, vt))

    @pl.when(c == pl.num_programs(1) - 1)
    def _():
        npos = past_len + lax.broadcasted_iota(jnp.int32, (1, LANES), 1)
        is_new = (npos <= qpos) & (npos < past_len + n_new)
        kn = _pad_rows(kn_ref[...], LANES).astype(BF16)
        vn = _pad_rows(vn_ref[...], LANES).astype(BF16)
        blk_new = past_len // SEL_BLOCK
        lane_s = lax.broadcasted_iota(jnp.int32, selm.shape, 1)
        picked = jnp.sum(jnp.where(lane_s == blk_new, selm, 0.0), axis=-1, keepdims=True) > 0.5
        _online_update(m_ref, l_ref, acc_ref, _dot_nt(q, kn) - bias(npos), picked & is_new, lambda p: _dot(p, vn))
        o_sel = acc_ref[...] / jnp.maximum(l_ref[...], 1e-30)
        mw_ref[...] = jnp.full(mw_ref.shape, NEG_INF, F32)
        lw_ref[...] = jnp.zeros_like(lw_ref)
        accw_ref[...] = jnp.zeros_like(accw_ref)
        kwt = kwin_ref[...].astype(BF16)
        vwt = vwin_ref[...].astype(BF16)
        wpos = past_len - w_buf + lax.broadcasted_iota(jnp.int32, (1, w_buf), 1)
        dist = qpos - wpos
        valid = (dist >= 0) & (dist < WINDOW) & (wpos >= 0)
        _online_update(mw_ref, lw_ref, accw_ref, _dot(q, kwt) - bias(wpos), valid, lambda p: _dot_nt(p, vwt))
        kwn = _pad_rows(kwn_ref[...], LANES).astype(BF16)
        vwn = _pad_rows(vwn_ref[...], LANES).astype(BF16)
        _online_update(mw_ref, lw_ref, accw_ref, _dot_nt(q, kwn) - bias(npos), is_new & (qpos - npos < WINDOW),
                       lambda p: _dot(p, vwn))
        o_win = accw_ref[...] / jnp.maximum(lw_ref[...], 1e-30)
        g = gate_ref[...]
        o_ref[...] = g[:, 0:1] * ocmp_ref[...] + g[:, 1:2] * o_sel + g[:, 2:3] * o_win


def _nsa_sample(pt, qbd, sel, k_new, v_new, kwin_t, vwin_t, kw_new, vw_new, ocmp, gates, pool_kt, pool_vt,
                n_pages, past_len, n_new, layer):
    s = qbd.shape[0]
    page = pool_kt.shape[2]
    w_buf = kwin_t.shape[2]
    ppc = math.gcd(NSA_PAGES_PER_STEP, n_pages)
    nsbp = sel.shape[-1]
    nsb = _round_up(past_len + n_new, SEL_BLOCK) // SEL_BLOCK
    rows = HB * 8
    chunk = ppc * page
    nchunk = n_pages // ppc
    bpc = chunk // SEL_BLOCK
    assert bpc <= LANES
    emat = (jnp.arange(chunk)[None, :] // SEL_BLOCK == jnp.arange(LANES)[:, None]).astype(BF16)
    neg = ((sel[..., :past_len // SEL_BLOCK] - 1.0) * MASK_NEG).reshape(s, KVH, 1, 8, nchunk, bpc)
    neg = jnp.broadcast_to(neg, (s, KVH, GB, 8, nchunk, bpc)).transpose(0, 4, 1, 2, 3, 5).reshape(s, nchunk, rows, bpc)
    neg = jnp.pad(neg, ((0, 0), (0, 0), (0, 0), (0, LANES - bpc))).astype(BF16)
    per_seq = lambda shape: pl.BlockSpec((None,) + shape, lambda si, c, pt: (si,) + (0,) * len(shape))
    state = pl.BlockSpec((None, LANES, w_buf), lambda si, c, pt: (layer * s + si, 0, 0))
    scratch = [pltpu.VMEM((rows, 1), F32), pltpu.VMEM((rows, 1), F32), pltpu.VMEM((rows, LANES), F32)]
    return pl.pallas_call(
        functools.partial(_nsa_sample_kernel, ppc=ppc, page=page, past_len=past_len, n_new=n_new, nsb=nsb, w_buf=w_buf),
        out_shape=jax.ShapeDtypeStruct((s, rows, LANES), F32),
        grid_spec=pltpu.PrefetchScalarGridSpec(
            num_scalar_prefetch=1, grid=(s, n_pages // ppc),
            in_specs=[per_seq((rows, LANES)), per_seq((KVH, 8, nsbp)),
                      pl.BlockSpec((None, None, rows, LANES), lambda si, c, pt: (si, c, 0, 0)),
                      _const_spec((LANES, chunk)),
                      per_seq((8, LANES)), per_seq((8, LANES)), state, state, per_seq((8, LANES)), per_seq((8, LANES)),
                      per_seq((rows, LANES)), per_seq((rows, LANES))]
                     + _page_specs(n_pages, ppc, LANES, page) + _page_specs(n_pages, ppc, LANES, page),
            out_specs=per_seq((rows, LANES)),
            scratch_shapes=scratch + scratch),
        compiler_params=_cparams(("parallel", "arbitrary")),
        name="nsa_sample",
    )(pt, qbd, sel, neg, emat, k_new, v_new, kwin_t, vwin_t, kw_new, vw_new, ocmp, gates,
      *([pool_kt] * ppc), *([pool_vt] * ppc))


def _prep_cmp(w1, pe, b1, w2, b2):
    eye = jnp.eye(KVH, dtype=F32)
    w1b = jnp.einsum('jde,hk->jhdke', w1, eye).reshape(L_CMP * KVH * DB, KVH * CMP_HID).astype(BF16)
    pef = jnp.tile(pe[:, None, :], (1, KVH, 1)).reshape(1, L_CMP * KVH * DB)
    w2b = jnp.einsum('ed,hk->hekd', w2, eye).reshape(KVH * CMP_HID, KVH * DB).astype(BF16)
    return (w1b, pef, jnp.tile(b1, KVH).reshape(1, -1), w2b, jnp.tile(b2, KVH).reshape(1, -1))


def _prep_layer(l, w_in, norm1_g, norm2_g, qn_a, kn_a, hn_a, qn_b, kn_cmp, kn_sel, kn_win, cmp_k, cmp_v,
                w_up_a, w_up_b, w_out, w_group, b_group, w_router, b_router, w_e_gate, w_e_up, w_e_down):
    d = w_in.shape[1]
    wi = w_in[l]
    o_qb = 2 * HA * 2 * DA + HA * DVA
    o_kv = o_qb + HB * DB
    o_gb = o_kv + 6 * KVH * DB
    o_gm = o_gb + 3 * HB
    wqb = wi[:, o_qb:o_kv].reshape(d, KVH, GB, DB)
    slot = jnp.zeros((d, KVH, GB, KVH, DB), F32)
    for kvh in range(KVH):
        slot = slot.at[:, kvh, :, kvh, :].set(wqb[:, kvh])
    wqb_ext = slot.reshape(d, HB * LANES)
    wgb = jnp.pad(wi[:, o_gb:o_gm], ((0, 0), (0, LANES - 3 * HB)))
    w_all = jnp.concatenate([wi[:, :o_qb], wqb_ext, wi[:, o_kv:o_gb], wgb, wi[:, o_gm:]], axis=1).astype(BF16)
    assert w_all.shape[1] == _W_COLS
    gq = jnp.zeros((KVH, GB, KVH, DB), F32)
    for kvh in range(KVH):
        gq = gq.at[kvh, :, kvh, :].set(jnp.broadcast_to(qn_b[l], (GB, DB)))
    ones = (jnp.arange(256)[:, None] // 64 == jnp.arange(256)[None, :] // 64).astype(BF16)
    w_r = jnp.concatenate([w_group[l], w_router[l].reshape(d, N_EXPERTS)], axis=1)
    w_r = jnp.pad(w_r, ((0, 0), (0, LANES - w_r.shape[1])))
    w_r_hi = w_r.astype(BF16)

    def by_group(w):
        w = w.reshape(N_GROUPS, EXPERTS_PER_GROUP, d, D_EXPERT).transpose(0, 2, 1, 3)
        return w.reshape(N_GROUPS, d, EXPERTS_PER_GROUP * D_EXPERT).astype(BF16)
    b_r = jnp.pad(jnp.concatenate([b_group[l], b_router[l].reshape(-1)]), (0, LANES - N_GROUPS - N_EXPERTS))
    return {
        "norm1_g": norm1_g[l].reshape(1, d), "norm2_g": norm2_g[l].reshape(1, d), "w_in": w_all,
        "g_qa": jnp.tile(qn_a[l], HA * 2).reshape(1, -1), "g_ka": jnp.tile(kn_a[l], HA * 2).reshape(1, -1),
        "g_qb": gq.reshape(1, -1), "g_ks": jnp.tile(kn_sel[l], KVH).reshape(1, -1),
        "g_kw": jnp.tile(kn_win[l], KVH).reshape(1, -1), "g_kc": jnp.tile(kn_cmp[l], KVH).reshape(1, -1),
        "hn_a": hn_a[l].reshape(1, DVA), "ones": ones,
        "cmp_k": _prep_cmp(*[a[l] for a in cmp_k]), "cmp_v": _prep_cmp(*[a[l] for a in cmp_v]),
        "w_up_a": w_up_a[l].astype(BF16), "w_up_b": w_up_b[l].astype(BF16), "w_out": w_out[l].astype(BF16),
        "w_r_hi": w_r_hi, "w_r_lo": (w_r - w_r_hi.astype(F32)).astype(BF16), "b_r": b_r.reshape(1, LANES),
        "w_gate": by_group(w_e_gate[l]), "w_up": by_group(w_e_up[l]),
        "w_down": w_e_down[l].reshape(N_GROUPS, EXPERTS_PER_GROUP * D_EXPERT, d).astype(BF16),
    }


def kernel(x_prompt, x_sample, cache_diff_k, cache_diff_v, cache_cmp_k, cache_cmp_v, cache_sel_k, cache_sel_v, state_win_k, state_win_v, page_table, c_prompt, c_sample, w_ada, b_ada, norm1_g, norm2_g, w_in, qn_a, kn_a, hn_a, lam_q1, lam_k1, lam_q2, lam_k2, qn_b, kn_cmp, kn_sel, kn_win, cmp_pe_k, cmp_w1_k, cmp_b1_k, cmp_w2_k, cmp_b2_k, cmp_pe_v, cmp_w1_v, cmp_b1_v, cmp_w2_v, cmp_b2_v, w_up_a, w_up_b, w_out, w_group, b_group, w_router, b_router, w_e_gate, w_e_up, w_e_down):
    b, t, d = x_prompt.shape
    s, n_new, _ = x_sample.shape
    depth, n_phys, page = cache_diff_k.shape[:3]
    n_pages = page_table.shape[1]
    past_len = n_pages * page
    w_buf = state_win_k.shape[2]
    assert n_new <= 8 and past_len % SEL_BLOCK == 0 and t % LANES == 0

    n_flat = depth * n_phys

    def page_t(p):
        nd = p.ndim
        return p.transpose(0, 1, *range(3, nd), 2).reshape(n_flat, -1, page)

    pool_diff_kt = page_t(cache_diff_k)
    pool_diff_v = cache_diff_v.reshape(n_flat, page * HA, DVA)
    pool_cmp_k = cache_cmp_k.reshape(n_flat, page, KVH * DB)
    pool_cmp_v = cache_cmp_v.reshape(n_flat, page, KVH * DB)
    pool_sel_kt, pool_sel_vt = page_t(cache_sel_k), page_t(cache_sel_v)
    win_kt = state_win_k.transpose(0, 1, 3, 4, 2).reshape(depth * s, KVH * DB, w_buf)
    win_vt = state_win_v.transpose(0, 1, 3, 4, 2).reshape(depth * s, KVH * DB, w_buf)
    slopes_a = jnp.asarray(SLOPES_A, F32)

    xp = x_prompt.reshape(b * t, d)
    xs = x_sample.reshape(s * n_new, d)
    c_all = jnp.concatenate([c_prompt, c_sample], axis=0)
    new_p = [[] for _ in range(8)]
    new_s = [[] for _ in range(8)]
    for l in range(depth):
        lambda_init = 0.8 - 0.6 * math.exp(-0.3 * l)
        out_scale = 1.0 - lambda_init
        wp = _prep_layer(l, w_in, norm1_g, norm2_g, qn_a, kn_a, hn_a, qn_b, kn_cmp, kn_sel, kn_win,
                         (cmp_w1_k, cmp_pe_k, cmp_b1_k, cmp_w2_k, cmp_b2_k),
                         (cmp_w1_v, cmp_pe_v, cmp_b1_v, cmp_w2_v, cmp_b2_v),
                         w_up_a, w_up_b, w_out, w_group, b_group, w_router, b_router, w_e_gate, w_e_up, w_e_down)
        mod, lam_row = _ada(c_all, w_ada[l], b_ada[l], lam_q1[l], lam_k1[l], lam_q2[l], lam_k2[l], lambda_init)
        lam = lam_row[0, :1]
        mods_p = [m.reshape(b, 1, d) for m in jnp.split(mod[:b], 6, axis=-1)]
        mods_s = [jnp.repeat(m, n_new, axis=0) for m in jnp.split(mod[b:], 6, axis=-1)]

        (qa, ka, va, qb, kc, vc, ks, vs, kw, vw, gbs, gm) = _proj(xp, mods_p[1], mods_p[0], False, t, wp)
        r3 = lambda a: a.reshape(b, t, -1)
        oa = _diff_prompt(r3(qa), r3(ka), r3(va), slopes_a, lam, wp["hn_a"], out_scale)
        ocmp, sel = _cmp_prompt(r3(qb), r3(kc), r3(vc), wp)
        ob = _nsa_prompt(r3(qb), r3(ks), r3(vs), r3(kw), r3(vw), sel, ocmp, r3(gbs))
        x1, h2, cw = _merge(oa.reshape(b * t, -1), ob.reshape(b * t, -1), gm, xp, mods_p[2], mods_p[4], mods_p[3],
                            False, t, wp)
        xp = _moe(h2, cw, x1, mods_p[5], False, t, wp)
        wp_rows = min(WINDOW, t)
        rows_p = [ka.reshape(b, t, HA, 2, DA), va.reshape(b, t, HA, DVA)] + \
                 [a.reshape(b, t, KVH, DB) for a in (kc, vc, ks, vs)] + \
                 [a.reshape(b, t, KVH, DB)[:, -wp_rows:] for a in (kw, vw)]

        (qa, ka, va, qb, kc, vc, ks, vs, kw, vw, gbs, gm) = _proj(xs, mods_s[1], mods_s[0], True, n_new, wp)
        pt = (page_table + l * n_phys).reshape(-1).astype(jnp.int32)
        pad8 = lambda a: jnp.pad(a.reshape(s, n_new, -1), ((0, 0), (0, 8 - n_new), (0, 0)))
        qa8 = pad8(qa)
        own = (jnp.arange(HA * 2 * DA)[None, :] // DA == jnp.arange(HA * 2)[:, None]).astype(BF16)
        qbd = (qa8[:, None, :, :] * own[None, :, None, :]).reshape(s, HA * 2 * 8, HA * 2 * DA)
        oa_s = _diff_sample(pt, lam, qbd, pad8(ka), pad8(va), wp["hn_a"], pool_diff_kt, pool_diff_v,
                            n_pages, past_len, n_new, out_scale)
        oa_s = oa_s[:, :n_new].reshape(s * n_new, HA * DVA)
        q_heads = pad8(qb).reshape(s, 8, HB, LANES).transpose(0, 2, 1, 3)
        ocmp_s, sel_s = _cmp_sample(pt, q_heads.reshape(s, HB * 8, LANES), pad8(kc), pad8(vc), pool_cmp_k, pool_cmp_v, n_pages, past_len, n_new, wp)
        kwin = jnp.concatenate([state_win_k[l].reshape(s, w_buf, -1), kw.reshape(s, n_new, -1)], axis=1)
        vwin = jnp.concatenate([state_win_v[l].reshape(s, w_buf, -1), vw.reshape(s, n_new, -1)], axis=1)
        gates = pad8(gbs)[:, :, :3 * HB].reshape(s, 8, HB, 3).transpose(0, 2, 1, 3).reshape(s, HB * 8, 3)
        gates = jnp.pad(gates, ((0, 0), (0, 0), (0, LANES - 3)))
        ob_s = _nsa_sample(pt, q_heads.reshape(s, HB * 8, LANES), sel_s, pad8(ks), pad8(vs), win_kt, win_vt,
                           pad8(kw), pad8(vw), ocmp_s.reshape(s, HB * 8, LANES), gates, pool_sel_kt, pool_sel_vt,
                           n_pages, past_len, n_new, l)
        ob_s = ob_s.reshape(s, KVH, GB, 8, KVH, DB)
        ob_s = jnp.stack([ob_s[:, kvh, :, :, kvh, :] for kvh in range(KVH)], axis=1)
        ob_s = ob_s.transpose(0, 3, 1, 2, 4)[:, :n_new].reshape(s * n_new, HB * DB).astype(BF16)
        x1, h2, cw = _merge(oa_s, ob_s, gm, xs, mods_s[2], mods_s[4], mods_s[3], True, n_new, wp)
        xs = _moe(h2, cw, x1, mods_s[5], True, n_new, wp)
        rows_s = [ka.reshape(s, n_new, HA, 2, DA), va.reshape(s, n_new, HA, DVA)] + \
                 [a.reshape(s, n_new, KVH, DB) for a in (kc, vc, ks, vs)] + \
                 [kwin[:, -w_buf:].reshape(s, w_buf, KVH, DB), vwin[:, -w_buf:].reshape(s, w_buf, KVH, DB)]
        for i in range(8):
            new_p[i].append(rows_p[i])
            new_s[i].append(rows_s[i])

    outs_p = [jnp.stack(a) for a in new_p]
    outs_s = [jnp.stack(a) for a in new_s]
    return (xp.reshape(b, t, d), xs.reshape(s, n_new, d), *outs_p, *outs_s)
```

```python
import functools
import math

import jax
import jax.numpy as jnp
from jax import lax
from jax.experimental import pallas as pl
from jax.experimental.pallas import tpu as pltpu

F32 = jnp.float32
BF16 = jnp.bfloat16

HA, DA, DVA = 4, 64, 128
HB, KVH, GB, DB = 8, 2, 4, 64
L_CMP, CMP_STRIDE, CMP_HID = 32, 16, 128
SEL_BLOCK, N_SEL, WINDOW = 64, 8, 512
N_GROUPS, EXPERTS_PER_GROUP, N_EXPERTS, D_EXPERT = 4, 4, 16, 256
RMS_EPS = 1e-6
NEG_INF = -1e30
BIG = 1e9
REMOVED = -3e38
MASK_NEG = 1e30
AUG_POS = DB - 2
SLOPES_A = tuple(2.0 ** (-8.0 * (i + 1) / HA) for i in range(HA))
SLOPES_B = tuple(2.0 ** (-8.0 * (i + 1) / HB) for i in range(HB))
QK_SCALE = 0.125

LANES = 128
VMEM_LIMIT = 56 * 1024 * 1024

ROW_TILE = 256
MOE_ROW_TILE = 512
DIFF_TQ = 256
NSA_TQ = 128
NSA_TK = 256
CMP_TQ = 256
DIFF_PAGES_PER_STEP = 16
NSA_PAGES_PER_STEP = 32


def _cparams(sem):
    return pltpu.CompilerParams(dimension_semantics=sem, vmem_limit_bytes=VMEM_LIMIT)


def _const_spec(shape):
    nd = len(shape)
    return pl.BlockSpec(shape, lambda *_: (0,) * nd)


def _round_up(x, m):
    return -(-x // m) * m


def _split3(x):
    x1 = x.astype(BF16)
    r = x - x1.astype(F32)
    x2 = r.astype(BF16)
    x3 = (r - x2.astype(F32)).astype(BF16)
    return x1, x2, x3


def _dot(a, b):
    return jnp.dot(a, b, preferred_element_type=F32)


def _dot_nt(a, b):
    return lax.dot_general(a, b, (((1,), (1,)), ((), ())), preferred_element_type=F32)


def _gelu_tanh(x):
    return 0.5 * x * (1.0 + jnp.tanh(0.7978845608028654 * (x + 0.044715 * x * x * x)))


def _topk_mask(imp, k):
    width = imp.shape[-1]
    lane = lax.broadcasted_iota(jnp.int32, imp.shape, 1).astype(F32)
    sel = jnp.zeros(imp.shape, F32)
    work = imp
    for _ in range(k):
        m = jnp.max(work, axis=-1, keepdims=True)
        idx = jnp.min(jnp.where(work == m, lane, float(width)), axis=-1, keepdims=True)
        hit = lane == idx
        sel = jnp.where(hit, 1.0, sel)
        work = jnp.where(hit, REMOVED, work)
    return sel


def _ada_kernel(c_ref, w_ref, b_ref, q1_ref, k1_ref, q2_ref, k2_ref, o_ref, lam_ref, *, lambda_init):
    c = c_ref[...]
    s = c * jax.nn.sigmoid(c)
    s1, s2, s3 = _split3(s)
    w1, w2, w3 = _split3(w_ref[...])
    acc = _dot(s1, w1) + (_dot(s1, w2) + _dot(s2, w1)) + (_dot(s1, w3) + _dot(s2, w2) + _dot(s3, w1))
    o_ref[...] = acc + b_ref[...]
    a = jnp.sum(q1_ref[...] * k1_ref[...], axis=-1, keepdims=True)
    b = jnp.sum(q2_ref[...] * k2_ref[...], axis=-1, keepdims=True)
    lam_ref[...] = jnp.broadcast_to(jnp.exp(a) - jnp.exp(b) + lambda_init, lam_ref.shape)


def _ada(c_all, w_ada, b_ada, lq1, lk1, lq2, lk2, lambda_init):
    n, d = c_all.shape
    n_out = w_ada.shape[1]
    tn = 512
    vec = pl.BlockSpec((1, DA), lambda j: (0, 0))
    return pl.pallas_call(
        functools.partial(_ada_kernel, lambda_init=lambda_init),
        out_shape=(jax.ShapeDtypeStruct((n, n_out), F32), jax.ShapeDtypeStruct((1, LANES), F32)),
        grid=(n_out // tn,),
        in_specs=[pl.BlockSpec((n, d), lambda j: (0, 0)), pl.BlockSpec((d, tn), lambda j: (0, j)),
                  pl.BlockSpec((1, tn), lambda j: (0, j)), vec, vec, vec, vec],
        out_specs=(pl.BlockSpec((n, tn), lambda j: (0, j)), pl.BlockSpec((1, LANES), lambda j: (0, 0))),
        compiler_params=_cparams(("arbitrary",)),
        name="ada",
    )(c_all, w_ada, b_ada.reshape(1, -1), lq1.reshape(1, -1), lk1.reshape(1, -1), lq2.reshape(1, -1), lk2.reshape(1, -1))


_W_QA, _W_KA, _W_VA, _W_QB = 0, 512, 1024, 1536
_W_KV6 = _W_QB + HB * LANES
_W_GB = _W_KV6 + 6 * 128
_W_GM = _W_GB + 128
_W_COLS = _W_GM + 2048


def _group_norm(p, gain, ones):
    n = p.shape[1]
    w = 256 if n % 256 == 0 else 128
    sq = (p * p).astype(BF16)
    parts = [_dot(sq[:, c:c + w], ones[:w, :w]) for c in range(0, n, w)]
    ss = parts[0] if len(parts) == 1 else jnp.concatenate(parts, axis=1)
    return p * lax.rsqrt(ss * (1.0 / 64.0) + RMS_EPS) * gain


def _proj_kernel(x_ref, sc_ref, sh_ref, ng_ref, w_ref, gqa_ref, gka_ref, gqb_ref, gks_ref, gkw_ref, ones_ref,
                 qa_ref, ka_ref, va_ref, qb_ref, kc_ref, vc_ref, ks_ref, vs_ref, kw_ref, vw_ref, gb_ref, gm_ref):
    x = x_ref[...]
    h = x * lax.rsqrt(jnp.mean(x * x, axis=-1, keepdims=True) + RMS_EPS) * ng_ref[...]
    h = h * (1.0 + sc_ref[...]) + sh_ref[...]
    hb = h.astype(BF16)
    ones = ones_ref[...]

    def seg(a, n):
        return _dot(hb, w_ref[:, a:a + n])

    qa_ref[...] = (_group_norm(seg(_W_QA, 512), gqa_ref[...], ones) * QK_SCALE).astype(BF16)
    ka_ref[...] = _group_norm(seg(_W_KA, 512), gka_ref[...], ones)
    va_ref[...] = seg(_W_VA, 512)
    qb_ref[...] = (_group_norm(seg(_W_QB, 1024), gqb_ref[...], ones) * QK_SCALE).astype(BF16)
    kc_ref[...] = seg(_W_KV6, 128)
    vc_ref[...] = seg(_W_KV6 + 128, 128)
    ks_ref[...] = _group_norm(seg(_W_KV6 + 256, 128), gks_ref[...], ones)
    vs_ref[...] = seg(_W_KV6 + 384, 128)
    kw_ref[...] = _group_norm(seg(_W_KV6 + 512, 128), gkw_ref[...], ones)
    vw_ref[...] = seg(_W_KV6 + 640, 128)
    gb_ref[...] = jax.nn.sigmoid(seg(_W_GB, 128))
    gm_ref[:, :1024] = jax.nn.sigmoid(seg(_W_GM, 1024)).astype(BF16)
    gm_ref[:, 1024:] = jax.nn.sigmoid(seg(_W_GM + 1024, 1024)).astype(BF16)


def _mod_spec(per_token, tm, bps, d):
    if per_token:
        return pl.BlockSpec((tm, d), lambda i, *_: (i, 0))
    return pl.BlockSpec((None, 1, d), lambda i, *_: (i // bps, 0, 0))


def _proj(x, sc, sh, per_token, rows_per_seq, wp):
    n, d = x.shape
    tm = min(ROW_TILE, n)
    bps = max(rows_per_seq // tm, 1)
    mod = _mod_spec(per_token, tm, bps, d)

    def row(width):
        return pl.BlockSpec((tm, width), lambda i: (i, 0))

    widths = [(512, BF16), (512, F32), (512, F32), (1024, BF16)] + [(128, F32)] * 6 + [(128, F32), (2048, BF16)]
    return pl.pallas_call(
        _proj_kernel,
        out_shape=tuple(jax.ShapeDtypeStruct((n, w), dt) for w, dt in widths),
        grid=(n // tm,),
        in_specs=[row(d), mod, mod, _const_spec((1, d)), _const_spec((d, _W_COLS)),
                  _const_spec((1, 512)), _const_spec((1, 512)), _const_spec((1, 1024)),
                  _const_spec((1, 128)), _const_spec((1, 128)), _const_spec((256, 256))],
        out_specs=tuple(row(w) for w, _ in widths),
        compiler_params=_cparams(("parallel",)),
        name="proj",
    )(x, sc, sh, wp["norm1_g"], wp["w_in"], wp["g_qa"], wp["g_ka"], wp["g_qb"], wp["g_ks"], wp["g_kw"], wp["ones"])


POS_RADIX = 256


def _pos_lanes(n_keys):
    k = jnp.arange(n_keys)
    return jnp.stack([k // POS_RADIX, k % POS_RADIX], axis=1).astype(F32)


def _flash_step(m, acc, s, valid, v_ext):
    if valid is not None:
        s = jnp.where(valid, s, NEG_INF)
    m_new = jnp.maximum(m, jnp.max(s, axis=-1, keepdims=True))
    alpha = jnp.exp(m - m_new)
    p = jnp.exp(s - m_new)
    acc = alpha * acc + _dot(p.astype(BF16), v_ext)
    return m_new, acc


DIFF_HEADS_PER_STEP = 4


def _diff_prompt_kernel(slope_ref, lam_ref, q_ref, k_ref, v_ref, pos_ref, hn_ref, o_ref, ka0, ka1, vext,
                        *, tq, out_scale):
    hp = pl.program_id(1)
    i = pl.program_id(2)
    lam = lam_ref[0]
    nh = DIFF_HEADS_PER_STEP

    @pl.when(i == 0)
    def _():
        pos = pos_ref[...]
        lane = lax.broadcasted_iota(jnp.int32, pos.shape, 1)
        for hh in range(nh):
            k = k_ref[:, hh * 2 * DA:(hh + 1) * 2 * DA]
            ka0[hh] = jnp.where(lane < DA, k, pos).astype(BF16)
            ka1[hh] = jnp.where(lane >= DA, k, pos).astype(BF16)
            vext[hh, :, :DVA] = v_ref[:, hh * DVA:(hh + 1) * DVA].astype(BF16)
            vext[hh, :, DVA:] = jnp.ones((vext.shape[1], DVA), BF16)

    lane = lax.broadcasted_iota(jnp.int32, (tq, 2 * DA), 1)
    half = lane % DA
    chains = []
    for hh in range(nh):
        slope = slope_ref[hp * nh + hh]
        q = q_ref[:, hh * 2 * DA:(hh + 1) * 2 * DA].astype(F32)
        digits = jnp.where(half == 0, slope * POS_RADIX, jnp.where(half == 1, slope, 0.0))
        chains.append((hh, ka0, jnp.where(lane < DA, q, digits).astype(BF16)))
        chains.append((hh, ka1, jnp.where(lane >= DA, q, digits).astype(BF16)))
    r = lax.broadcasted_iota(jnp.int32, (tq, tq), 0)
    c = lax.broadcasted_iota(jnp.int32, (tq, tq), 1)
    causal = r >= c

    def block(j, carry, valid):
        start = pl.multiple_of(j * tq, tq)
        out = []
        for (hh, kaug, qm), (m, acc) in zip(chains, carry):
            s = _dot_nt(qm, kaug[hh, pl.ds(start, tq), :])
            out.append(_flash_step(m, acc, s, valid, vext[hh, pl.ds(start, tq), :]))
        return tuple(out)

    init = tuple((jnp.full((tq, 1), NEG_INF, F32), jnp.zeros((tq, 2 * DVA), F32)) for _ in chains)
    carry = lax.fori_loop(0, i, lambda j, cr: block(j, cr, None), init)
    carry = block(i, carry, causal)
    outs = [acc[:, :DVA] / acc[:, DVA:] for _, acc in carry]
    for hh in range(nh):
        o = outs[2 * hh] - lam * outs[2 * hh + 1]
        o = o * lax.rsqrt(jnp.mean(o * o, axis=-1, keepdims=True) + RMS_EPS) * hn_ref[...]
        o_ref[:, hh * DVA:(hh + 1) * DVA] = (o * out_scale).astype(BF16)


def _diff_prompt(qa, ka, va, slopes, lam, hn, out_scale):
    b, t, _ = qa.shape
    tq = min(DIFF_TQ, t)
    nh = DIFF_HEADS_PER_STEP
    assert t <= POS_RADIX * POS_RADIX and HA % nh == 0
    smem = pl.BlockSpec(memory_space=pltpu.SMEM)
    pos = jnp.pad(_pos_lanes(t), ((0, 0), (0, DA - 2)))
    pos = jnp.concatenate([pos, pos], axis=1)
    return pl.pallas_call(
        functools.partial(_diff_prompt_kernel, tq=tq, out_scale=out_scale),
        out_shape=jax.ShapeDtypeStruct((b, t, HA * DVA), BF16),
        grid=(b, HA // nh, t // tq),
        in_specs=[smem, smem,
                  pl.BlockSpec((None, tq, nh * 2 * DA), lambda bi, h, i: (bi, i, h)),
                  pl.BlockSpec((None, t, nh * 2 * DA), lambda bi, h, i: (bi, 0, h)),
                  pl.BlockSpec((None, t, nh * DVA), lambda bi, h, i: (bi, 0, h)),
                  pl.BlockSpec((t, 2 * DA), lambda bi, h, i: (0, 0)),
                  pl.BlockSpec((1, DVA), lambda bi, h, i: (0, 0))],
        out_specs=pl.BlockSpec((None, tq, nh * DVA), lambda bi, h, i: (bi, i, h)),
        scratch_shapes=[pltpu.VMEM((nh, t, 2 * DA), BF16), pltpu.VMEM((nh, t, 2 * DA), BF16),
                        pltpu.VMEM((nh, t, 2 * DVA), BF16)],
        compiler_params=_cparams(("parallel", "parallel", "arbitrary")),
        name="diff_prompt",
    )(slopes, lam, qa, ka, va, pos, hn)


def _compress(buf_ref, m_rows, w1_ref, pe_ref, b1_ref, w2_ref, b2_ref):
    acc = jnp.zeros((m_rows, KVH * CMP_HID), F32)
    group = 8
    for jg in range(L_CMP // group):
        cols = slice(jg * group * LANES, (jg + 1) * group * LANES)
        xs = [buf_ref[pl.ds(jg * group + jj, m_rows, stride=CMP_STRIDE), :] for jj in range(group)]
        x = jnp.concatenate(xs, axis=1) + pe_ref[:, cols]
        acc = acc + _dot(x.astype(BF16), w1_ref[cols, :])
    hid = _gelu_tanh(acc + b1_ref[...])
    return _dot(hid.astype(BF16), w2_ref[...]) + b2_ref[...]


def _cmp_attention(q_ext, kcmp_b, vcmp_b, qpos, slopes_h, gsum, nsb, n_q_rows):
    m_rows = kcmp_b.shape[0]
    cend = (lax.broadcasted_iota(jnp.int32, (1, m_rows), 1) * CMP_STRIDE + (L_CMP - 1))
    visible = cend <= qpos
    dist = (qpos - cend).astype(F32)
    outs = []
    psum = jnp.zeros((n_q_rows, m_rows), F32)
    for g in range(GB):
        s = _dot_nt(q_ext[g], kcmp_b) - slopes_h[g] * dist
        s = jnp.where(visible, s, NEG_INF)
        mx = jnp.max(s, axis=-1, keepdims=True)
        e = jnp.where(visible, jnp.exp(s - mx), 0.0)
        p = e / jnp.maximum(jnp.sum(e, axis=-1, keepdims=True), 1e-30)
        outs.append(_dot(p.astype(BF16), vcmp_b))
        psum = psum + p
    return outs, _select_blocks(psum, qpos, gsum, nsb)


def _select_blocks(psum, qpos, gsum, nsb):
    p1, p2, p3 = _split3(psum)
    imp = _dot(p1, gsum) + _dot(p2, gsum) + _dot(p3, gsum)
    blk = lax.broadcasted_iota(jnp.int32, (1, imp.shape[1]), 1)
    cur = qpos // SEL_BLOCK
    forced = (blk == 0) | (blk == cur) | (blk == cur - 1)
    imp = jnp.where(blk > cur, -BIG, imp)
    imp = jnp.where(forced, BIG, imp)
    imp = jnp.where(blk >= nsb, REMOVED, imp)
    return _topk_mask(imp, min(N_SEL, nsb))


def _cmp_prompt_kernel(q_ref, kc_ref, vc_ref, w1k_ref, pek_ref, b1k_ref, w2k_ref, b2k_ref,
                       w1v_ref, pev_ref, b1v_ref, w2v_ref, b2v_ref, kn_ref, ones_ref, gsum_ref,
                       o_ref, sel_ref, bufk, bufv, kcmp, vcmp, *, t, tq, m_rows, nsb):
    i = pl.program_id(1)

    @pl.when(i == 0)
    def _():
        for buf, src in ((bufk, kc_ref), (bufv, vc_ref)):
            buf[pl.ds(0, t), :] = src[...]
            buf[pl.ds(t, buf.shape[0] - t), :] = jnp.zeros((buf.shape[0] - t, LANES), F32)
        k = _compress(bufk, m_rows, w1k_ref, pek_ref, b1k_ref, w2k_ref, b2k_ref)
        kcmp[...] = _group_norm(k, kn_ref[...], ones_ref[...]).astype(BF16)
        vcmp[...] = _compress(bufv, m_rows, w1v_ref, pev_ref, b1v_ref, w2v_ref, b2v_ref).astype(BF16)

    qpos = i * tq + lax.broadcasted_iota(jnp.int32, (tq, 1), 0)
    kcmp_b = kcmp[...]
    vcmp_b = vcmp[...]
    gsum = gsum_ref[...]
    for kvh in range(KVH):
        q_ext = [q_ref[:, (kvh * GB + g) * LANES:(kvh * GB + g + 1) * LANES] for g in range(GB)]
        outs, sel = _cmp_attention(q_ext, kcmp_b, vcmp_b, qpos, SLOPES_B[kvh * GB:(kvh + 1) * GB], gsum, nsb, tq)
        for g in range(GB):
            o_ref[:, (kvh * GB + g) * LANES:(kvh * GB + g + 1) * LANES] = outs[g].astype(BF16)
        blk = lax.broadcasted_iota(jnp.int32, sel.shape, 1)
        neg = jnp.where(blk < nsb, (sel - 1.0) * MASK_NEG, 0.0)
        if kvh == 0:
            neg = pltpu.roll(neg, DB, 1)
        sel_ref[kvh] = neg.astype(BF16)


def _cmp_prompt(qb_ext, kc, vc, wp):
    b, t, _ = qb_ext.shape
    tq = min(CMP_TQ, t)
    nc = t // CMP_STRIDE
    m_rows = _round_up(nc, LANES)
    nsb = t // SEL_BLOCK
    nsbp = _round_up(nsb, LANES)
    assert nsbp == LANES and nsb <= AUG_POS
    buf_rows = CMP_STRIDE * (m_rows - 1) + L_CMP
    gsum = ((jnp.arange(m_rows)[:, None] // (SEL_BLOCK // CMP_STRIDE) == jnp.arange(nsbp)[None, :])
            & (jnp.arange(m_rows)[:, None] < nc)).astype(BF16)
    kv = pl.BlockSpec((None, t, LANES), lambda bi, i: (bi, 0, 0))
    cw = [_const_spec((L_CMP * LANES, KVH * CMP_HID)), _const_spec((1, L_CMP * LANES)), _const_spec((1, KVH * CMP_HID)),
          _const_spec((KVH * CMP_HID, LANES)), _const_spec((1, LANES))]
    return pl.pallas_call(
        functools.partial(_cmp_prompt_kernel, t=t, tq=tq, m_rows=m_rows, nsb=nsb),
        out_shape=(jax.ShapeDtypeStruct((b, t, HB * LANES), BF16), jax.ShapeDtypeStruct((b, KVH, t, nsbp), BF16)),
        grid=(b, t // tq),
        in_specs=[pl.BlockSpec((None, tq, HB * LANES), lambda bi, i: (bi, i, 0)), kv, kv] + cw + cw
                 + [_const_spec((1, LANES)), _const_spec((256, 256)), _const_spec((m_rows, nsbp))],
        out_specs=(pl.BlockSpec((None, tq, HB * LANES), lambda bi, i: (bi, i, 0)),
                   pl.BlockSpec((None, KVH, tq, nsbp), lambda bi, i: (bi, 0, i, 0))),
        scratch_shapes=[pltpu.VMEM((buf_rows, LANES), F32), pltpu.VMEM((buf_rows, LANES), F32),
                        pltpu.VMEM((m_rows, LANES), BF16), pltpu.VMEM((m_rows, LANES), BF16)],
        compiler_params=_cparams(("parallel", "arbitrary")),
        name="cmp_prompt",
    )(qb_ext, kc, vc, *wp["cmp_k"], *wp["cmp_v"], wp["g_kc"], wp["ones"], gsum)


def _compact_pairs(ext, kvh):
    lane = lax.broadcasted_iota(jnp.int32, ext[0].shape, 1)
    pieces = []
    for pair in range(GB // 2):
        even, odd = ext[2 * pair], ext[2 * pair + 1]
        if kvh == 0:
            pieces.append(jnp.where(lane < DB, even, pltpu.roll(odd, DB, 1)))
        else:
            pieces.append(jnp.where(lane < DB, pltpu.roll(even, DB, 1), odd))
    return jnp.concatenate(pieces, axis=1)


def _nsa_prompt_kernel(q_ref, ks_ref, vs_ref, kw_ref, vw_ref, sel_ref, ocmp_ref, gb_ref, aug_ref, o_ref,
                       ka_s, va_s, ka_w, va_w, *, tq, tk):
    i = pl.program_id(1)
    rows = 2 * tq

    @pl.when(i == 0)
    def _():
        lane = lax.broadcasted_iota(jnp.int32, ks_ref.shape, 1)
        for kvh in range(KVH):
            own = (lane >= kvh * DB) & (lane < (kvh + 1) * DB)
            aug = aug_ref[kvh]
            ka_s[kvh] = jnp.where(own, ks_ref[...], aug).astype(BF16)
            ka_w[kvh] = jnp.where(own, kw_ref[...], aug).astype(BF16)
            va_s[kvh] = jnp.where(own, vs_ref[...], 1.0).astype(BF16)
            va_w[kvh] = jnp.where(own, vw_ref[...], 1.0).astype(BF16)

    r = lax.broadcasted_iota(jnp.int32, (rows, tk), 0) % tq
    c = lax.broadcasted_iota(jnp.int32, (rows, tk), 1)
    dbase = r - c
    qlane = lax.broadcasted_iota(jnp.int32, (tq, LANES), 1)
    gates = gb_ref[...]
    j_hi = (i * tq + tq - 1) // tk
    j_lo_win = jnp.maximum(i * tq - (WINDOW - 1), 0) // tk

    chains = []
    for kvh in range(KVH):
        base = (1 - kvh) * DB
        own_q = (qlane >= kvh * DB) & (qlane < (kvh + 1) * DB)
        negmask = sel_ref[kvh].astype(F32)
        q_sel, q_win = [], []
        for g in range(GB):
            head = kvh * GB + g
            slot = q_ref[:, head * LANES:(head + 1) * LANES].astype(F32)
            digits = jnp.where(qlane == base + AUG_POS, SLOPES_B[head] * POS_RADIX,
                               jnp.where(qlane == base + AUG_POS + 1, SLOPES_B[head], 0.0))
            q_win.append(jnp.where(own_q, slot, digits).astype(BF16))
            q_sel.append(jnp.where(own_q, slot, digits + negmask).astype(BF16))
        for g in range(0, GB, 2):
            chains.append((kvh, g, jnp.concatenate(q_sel[g:g + 2], axis=0), jnp.concatenate(q_win[g:g + 2], axis=0)))

    def sel_block(j, carry, valid=None):
        start = pl.multiple_of(j * tk, tk)
        out = []
        for (kvh, _, qs, _), (m, acc) in zip(chains, carry):
            s = _dot_nt(qs, ka_s[kvh, pl.ds(start, tk), :])
            out.append(_flash_step(m, acc, s, valid, va_s[kvh, pl.ds(start, tk), :]))
        return tuple(out)

    def win_block(j, carry):
        start = pl.multiple_of(j * tk, tk)
        dist = dbase - (j * tk - i * tq)
        valid = (dist >= 0) & (dist < WINDOW)
        out = []
        for (kvh, _, _, qw), (m, acc) in zip(chains, carry):
            s = _dot_nt(qw, ka_w[kvh, pl.ds(start, tk), :])
            out.append(_flash_step(m, acc, s, valid, va_w[kvh, pl.ds(start, tk), :]))
        return tuple(out)

    init = tuple((jnp.full((rows, 1), NEG_INF, F32), jnp.zeros((rows, LANES), F32)) for _ in chains)
    carry = lax.fori_loop(0, j_hi, sel_block, init)
    res_s = sel_block(j_hi, carry, dbase >= j_hi * tk - i * tq)
    res_w = lax.fori_loop(j_lo_win, j_hi + 1, win_block, init)

    def finish(acc):
        den = pltpu.roll(acc, DB, 1)
        return acc / jnp.maximum(den, 1e-30)

    ext = [[None] * GB for _ in range(KVH)]
    for (kvh, g0, _, _), (_, acc_s), (_, acc_w) in zip(chains, res_s, res_w):
        o_s, o_w = finish(acc_s), finish(acc_w)
        for gi in range(2):
            g = g0 + gi
            head = kvh * GB + g
            col = head * 3
            oc = ocmp_ref[:, head * LANES:(head + 1) * LANES].astype(F32)
            ext[kvh][g] = (gates[:, col:col + 1] * oc + gates[:, col + 1:col + 2] * o_s[gi * tq:(gi + 1) * tq]
                           + gates[:, col + 2:col + 3] * o_w[gi * tq:(gi + 1) * tq])
    for kvh in range(KVH):
        o_ref[:, kvh * GB * DB:(kvh + 1) * GB * DB] = _compact_pairs(ext[kvh], kvh).astype(BF16)


def _nsa_prompt(qb_ext, ks, vs, kw, vw, sel, ocmp, gbs):
    b, t, _ = qb_ext.shape
    tq = min(NSA_TQ, t)
    tk = min(NSA_TK, t)
    nsb = t // SEL_BLOCK
    assert tk % tq == 0 and nsb <= AUG_POS and t <= POS_RADIX * POS_RADIX
    half = jnp.concatenate([(jnp.arange(t)[:, None] // SEL_BLOCK == jnp.arange(AUG_POS)[None, :]).astype(F32),
                            _pos_lanes(t)], axis=1)
    zeros = jnp.zeros((t, DB), F32)
    aug = jnp.stack([jnp.concatenate([zeros, half], axis=1), jnp.concatenate([half, zeros], axis=1)])
    kv = pl.BlockSpec((None, t, LANES), lambda bi, i: (bi, 0, 0))
    scratch = pltpu.VMEM((KVH, t, LANES), BF16)
    return pl.pallas_call(
        functools.partial(_nsa_prompt_kernel, tq=tq, tk=tk),
        out_shape=jax.ShapeDtypeStruct((b, t, HB * DB), BF16),
        grid=(b, t // tq),
        in_specs=[pl.BlockSpec((None, tq, HB * LANES), lambda bi, i: (bi, i, 0)), kv, kv, kv, kv,
                  pl.BlockSpec((None, KVH, tq, LANES), lambda bi, i: (bi, 0, i, 0)),
                  pl.BlockSpec((None, tq, HB * LANES), lambda bi, i: (bi, i, 0)),
                  pl.BlockSpec((None, tq, LANES), lambda bi, i: (bi, i, 0)),
                  _const_spec((KVH, t, LANES))],
        out_specs=pl.BlockSpec((None, tq, HB * DB), lambda bi, i: (bi, i, 0)),
        scratch_shapes=[scratch, scratch, scratch, scratch],
        compiler_params=_cparams(("parallel", "arbitrary")),
        name="nsa_prompt",
    )(qb_ext, ks, vs, kw, vw, sel, ocmp, gbs, aug)


def _merge_kernel(oa_ref, ob_ref, gm_ref, x_ref, g1_ref, sc_ref, sh_ref, ng_ref, wa_ref, wb_ref, wo_ref,
                  wrh_ref, wrl_ref, br_ref, x1_ref, h2_ref, cw_ref):
    ya = _dot(oa_ref[...], wa_ref[...])
    yb = _dot(ob_ref[...], wb_ref[...])
    d = ya.shape[1]
    t = gm_ref[:, :d].astype(F32) * ya + gm_ref[:, d:].astype(F32) * yb
    mix = _dot(t.astype(BF16), wo_ref[...])
    x1 = x_ref[...] + g1_ref[...] * mix
    x1_ref[...] = x1
    h2 = x1 * lax.rsqrt(jnp.mean(x1 * x1, axis=-1, keepdims=True) + RMS_EPS) * ng_ref[...]
    h2 = h2 * (1.0 + sc_ref[...]) + sh_ref[...]
    h2_ref[...] = h2.astype(BF16)
    hh = h2.astype(BF16)
    hl = (h2 - hh.astype(F32)).astype(BF16)
    logits = _dot(hh, wrh_ref[...]) + (_dot(hh, wrl_ref[...]) + _dot(hl, wrh_ref[...])) + br_ref[...]
    lane = lax.broadcasted_iota(jnp.int32, logits.shape, 1)
    lane_f = lane.astype(F32)
    is_grp = lane < N_GROUPS
    lg = jnp.where(is_grp, logits, NEG_INF)
    gmax = jnp.max(lg, axis=-1, keepdims=True)
    gidx = jnp.min(jnp.where(lg == gmax, lane_f, float(LANES)), axis=-1, keepdims=True)
    p_grp = 1.0 / jnp.sum(jnp.where(is_grp, jnp.exp(lg - gmax), 0.0), axis=-1, keepdims=True)
    egrp = ((lane - N_GROUPS) // EXPERTS_PER_GROUP).astype(F32)
    in_grp = (lane >= N_GROUPS) & (lane < N_GROUPS + N_EXPERTS) & (egrp == gidx)
    le = jnp.where(in_grp, logits, REMOVED)
    v1 = jnp.max(le, axis=-1, keepdims=True)
    i1 = jnp.min(jnp.where(le == v1, lane_f, float(LANES)), axis=-1, keepdims=True)
    le2 = jnp.where(lane_f == i1, REMOVED, le)
    v2 = jnp.max(le2, axis=-1, keepdims=True)
    i2 = jnp.min(jnp.where(le2 == v2, lane_f, float(LANES)), axis=-1, keepdims=True)
    e2 = jnp.exp(v2 - v1)
    w1 = 1.0 / (1.0 + e2)
    cw_ref[...] = jnp.where(lane_f == i1, w1 * p_grp, jnp.where(lane_f == i2, e2 * w1 * p_grp, 0.0))


def _merge(oa, ob, gm, x, g1, sc2, sh2, per_token, rows_per_seq, wp):
    n, d = x.shape
    tm = min(ROW_TILE, n)
    bps = max(rows_per_seq // tm, 1)
    mod = _mod_spec(per_token, tm, bps, d)

    def row(width):
        return pl.BlockSpec((tm, width), lambda i: (i, 0))

    return pl.pallas_call(
        _merge_kernel,
        out_shape=(jax.ShapeDtypeStruct((n, d), F32), jax.ShapeDtypeStruct((n, d), BF16),
                   jax.ShapeDtypeStruct((n, LANES), F32)),
        grid=(n // tm,),
        in_specs=[row(HA * DVA), row(HB * DB), row(2 * d), row(d), mod, mod, mod, _const_spec((1, d)),
                  _const_spec((HA * DVA, d)), _const_spec((HB * DB, d)), _const_spec((d, d)),
                  _const_spec((d, LANES)), _const_spec((d, LANES)), _const_spec((1, LANES))],
        out_specs=(row(d), row(d), row(LANES)),
        compiler_params=_cparams(("parallel",)),
        name="merge",
    )(oa, ob, gm, x, g1, sc2, sh2, wp["norm2_g"], wp["w_up_a"], wp["w_up_b"], wp["w_out"],
      wp["w_r_hi"], wp["w_r_lo"], wp["b_r"])


def _moe_kernel(h_ref, cw_ref, x_ref, g2_ref, wg_ref, wu_ref, wd_ref, o_ref, acc_ref):
    grp = pl.program_id(1)

    @pl.when(grp == 0)
    def _():
        acc_ref[...] = jnp.zeros_like(acc_ref)

    lane = lax.broadcasted_iota(jnp.int32, cw_ref.shape, 1)
    cw_all = cw_ref[...]
    h = h_ref[...]
    a = _dot(h, wg_ref[...])
    u = _dot(h, wu_ref[...])
    cols = []
    for el in range(EXPERTS_PER_GROUP):
        cw = jnp.sum(jnp.where(lane == N_GROUPS + grp * EXPERTS_PER_GROUP + el, cw_all, 0.0), axis=-1, keepdims=True)
        ae = a[:, el * D_EXPERT:(el + 1) * D_EXPERT]
        cols.append((ae * jax.nn.sigmoid(ae) * u[:, el * D_EXPERT:(el + 1) * D_EXPERT] * cw).astype(BF16))
    acc_ref[...] += _dot(jnp.concatenate(cols, axis=1), wd_ref[...])

    @pl.when(grp == pl.num_programs(1) - 1)
    def _():
        o_ref[...] = x_ref[...] + g2_ref[...] * acc_ref[...]


def _moe(h2, cw, x1, g2, per_token, rows_per_seq, wp):
    n, d = x1.shape
    tm = min(MOE_ROW_TILE, n)
    bps = max(rows_per_seq // tm, 1)
    gw = EXPERTS_PER_GROUP * D_EXPERT
    if per_token:
        mod = pl.BlockSpec((tm, d), lambda i, e: (i, 0))
    else:
        mod = pl.BlockSpec((None, 1, d), lambda i, e: (i // bps, 0, 0))
    return pl.pallas_call(
        _moe_kernel,
        out_shape=jax.ShapeDtypeStruct((n, d), F32),
        grid=(n // tm, N_GROUPS),
        in_specs=[pl.BlockSpec((tm, d), lambda i, e: (i, 0)), pl.BlockSpec((tm, LANES), lambda i, e: (i, 0)),
                  pl.BlockSpec((tm, d), lambda i, e: (i, 0)), mod,
                  pl.BlockSpec((None, d, gw), lambda i, e: (e, 0, 0)),
                  pl.BlockSpec((None, d, gw), lambda i, e: (e, 0, 0)),
                  pl.BlockSpec((None, gw, d), lambda i, e: (e, 0, 0))],
        out_specs=pl.BlockSpec((tm, d), lambda i, e: (i, 0)),
        scratch_shapes=[pltpu.VMEM((tm, d), F32)],
        compiler_params=_cparams(("parallel", "arbitrary")),
        name="moe",
    )(h2, cw, x1, g2, wp["w_gate"], wp["w_up"], wp["w_down"])


def _page_specs(n_pages, ppc, rows, width):
    def spec(p):
        return pl.BlockSpec((None, rows, width), lambda s, c, pt: (pt[s * n_pages + c * ppc + p], 0, 0))
    return [spec(p) for p in range(ppc)]


def _pad_rows(x, rows):
    return jnp.concatenate([x, jnp.zeros((rows - x.shape[0], x.shape[1]), x.dtype)], axis=0)


def _online_update(m_ref, l_ref, acc_ref, s, valid, pv_fn):
    if valid is not None:
        s = jnp.where(valid, s, NEG_INF)
    m_old = m_ref[...]
    m_new = jnp.maximum(m_old, jnp.max(s, axis=-1, keepdims=True))
    alpha = jnp.exp(m_old - m_new)
    p = jnp.exp(s - m_new)
    if valid is not None:
        p = jnp.where(valid, p, 0.0)
    m_ref[...] = m_new
    l_ref[...] = alpha * l_ref[...] + jnp.sum(p, axis=-1, keepdims=True)
    acc_ref[...] = alpha * acc_ref[...] + pv_fn(p.astype(BF16))


def _diff_sample_kernel(pt_ref, lam_ref, q_ref, kn_ref, vn_ref, hn_ref, *rest, ppc, page, past_len, n_new, out_scale):
    k_refs, v_refs = rest[:ppc], rest[ppc:2 * ppc]
    o_ref, m_ref, l_ref, acc_ref = rest[2 * ppc:]
    c = pl.program_id(1)
    rows = HA * 2 * 8
    hrows = 2 * 8
    lam = lam_ref[0]

    @pl.when(c == 0)
    def _():
        m_ref[...] = jnp.full(m_ref.shape, NEG_INF, F32)
        l_ref[...] = jnp.zeros_like(l_ref)
        acc_ref[...] = jnp.zeros_like(acc_ref)

    ridx = lax.broadcasted_iota(jnp.int32, (rows, 1), 0)
    qpos = past_len + ridx % 8
    slope = jnp.zeros((rows, 1), F32)
    for h in range(HA):
        slope = jnp.where(ridx // hrows == h, SLOPES_A[h], slope)
    q = q_ref[...]

    def per_head(p, v_heads):
        return jnp.concatenate([_dot(p[h * hrows:(h + 1) * hrows], v_heads[h]) for h in range(HA)], axis=0)

    chunk = ppc * page
    kt = jnp.concatenate([r[...] for r in k_refs], axis=1).astype(BF16)
    v_heads = [jnp.concatenate([r[pl.ds(h, page, stride=HA), :] for r in v_refs], axis=0).astype(BF16)
               for h in range(HA)]
    kpos = c * chunk + lax.broadcasted_iota(jnp.int32, (1, chunk), 1)
    s = _dot(q, kt) - slope * (qpos - kpos).astype(F32)
    _online_update(m_ref, l_ref, acc_ref, s, None, lambda p: per_head(p, v_heads))

    @pl.when(c == pl.num_programs(1) - 1)
    def _():
        kn = _pad_rows(kn_ref[...], LANES).astype(BF16)
        vn = _pad_rows(vn_ref[...], LANES).astype(BF16)
        vn_heads = [vn[:, h * DVA:(h + 1) * DVA] for h in range(HA)]
        npos = past_len + lax.broadcasted_iota(jnp.int32, (1, LANES), 1)
        s_new = _dot_nt(q, kn) - slope * (qpos - npos).astype(F32)
        _online_update(m_ref, l_ref, acc_ref, s_new, (npos <= qpos) & (npos < past_len + n_new),
                       lambda p: per_head(p, vn_heads))
        o_all = acc_ref[...] / l_ref[...]
        for h in range(HA):
            r0 = h * hrows
            o = o_all[r0:r0 + 8] - lam * o_all[r0 + 8:r0 + 16]
            o = o * lax.rsqrt(jnp.mean(o * o, axis=-1, keepdims=True) + RMS_EPS) * hn_ref[...]
            o_ref[:, h * DVA:(h + 1) * DVA] = (o * out_scale).astype(BF16)


def _diff_sample(pt, lam, qbd, k_new, v_new, hn, pool_kt, pool_v, n_pages, past_len, n_new, out_scale):
    s = qbd.shape[0]
    page = pool_kt.shape[2]
    ppc = math.gcd(DIFF_PAGES_PER_STEP, n_pages)
    rows = HA * 2 * 8
    per_seq = lambda shape: pl.BlockSpec((None,) + shape, lambda si, c, pt: (si, 0, 0))
    return pl.pallas_call(
        functools.partial(_diff_sample_kernel, ppc=ppc, page=page, past_len=past_len, n_new=n_new, out_scale=out_scale),
        out_shape=jax.ShapeDtypeStruct((s, 8, HA * DVA), BF16),
        grid_spec=pltpu.PrefetchScalarGridSpec(
            num_scalar_prefetch=1, grid=(s, n_pages // ppc),
            in_specs=[pl.BlockSpec(memory_space=pltpu.SMEM), per_seq((rows, HA * 2 * DA)),
                      per_seq((8, HA * 2 * DA)), per_seq((8, HA * DVA)), _const_spec((1, DVA))]
                     + _page_specs(n_pages, ppc, HA * 2 * DA, page) + _page_specs(n_pages, ppc, page * HA, DVA),
            out_specs=per_seq((8, HA * DVA)),
            scratch_shapes=[pltpu.VMEM((rows, 1), F32), pltpu.VMEM((rows, 1), F32), pltpu.VMEM((rows, DVA), F32)]),
        compiler_params=_cparams(("parallel", "arbitrary")),
        name="diff_sample",
    )(pt, lam, qbd, k_new, v_new, hn, *([pool_kt] * ppc), *([pool_v] * ppc))


def _cmp_sample_kernel(pt_ref, q_ref, kn_ref, vn_ref, w1k_ref, pek_ref, b1k_ref, w2k_ref, b2k_ref,
                       w1v_ref, pev_ref, b1v_ref, w2v_ref, b2v_ref, kng_ref, ones_ref, gsum_ref, *rest,
                       ppc, page, past_len, m_rows, mc_rows, nsb):
    k_refs, v_refs = rest[:ppc], rest[ppc:2 * ppc]
    o_ref, sel_ref, bufk, bufv = rest[2 * ppc:]
    c = pl.program_id(1)

    @pl.when(c == 0)
    def _():
        tail = bufk.shape[0] - past_len
        for buf, new in ((bufk, kn_ref), (bufv, vn_ref)):
            buf[pl.ds(past_len, tail), :] = jnp.zeros((tail, LANES), F32)
            buf[pl.ds(past_len, 8), :] = new[...]

    base = c * (ppc * page)
    for p in range(ppc):
        start = pl.multiple_of(base + p * page, page)
        bufk[pl.ds(start, page), :] = k_refs[p][...].T
        bufv[pl.ds(start, page), :] = v_refs[p][...].T

    @pl.when(c == pl.num_programs(1) - 1)
    def _():
        pad = jnp.zeros((m_rows - mc_rows, LANES), F32)
        k = _compress(bufk, mc_rows, w1k_ref, pek_ref, b1k_ref, w2k_ref, b2k_ref)
        k = _group_norm(k, kng_ref[...], ones_ref[...])
        kcmp_b = jnp.concatenate([k, pad], axis=0).astype(BF16)
        v = _compress(bufv, mc_rows, w1v_ref, pev_ref, b1v_ref, w2v_ref, b2v_ref)
        vcmp_b = jnp.concatenate([v, pad], axis=0).astype(BF16)
        rows = HB * 8
        ridx = lax.broadcasted_iota(jnp.int32, (rows, 1), 0)
        qpos = past_len + ridx % 8
        slope = jnp.zeros((rows, 1), F32)
        for h in range(HB):
            slope = jnp.where(ridx // 8 == h, SLOPES_B[h], slope)
        cend = lax.broadcasted_iota(jnp.int32, (1, m_rows), 1) * CMP_STRIDE + (L_CMP - 1)
        visible = cend <= qpos
        s = _dot_nt(q_ref[...], kcmp_b) - slope * (qpos - cend).astype(F32)
        s = jnp.where(visible, s, NEG_INF)
        e = jnp.where(visible, jnp.exp(s - jnp.max(s, axis=-1, keepdims=True)), 0.0)
        p = e / jnp.maximum(jnp.sum(e, axis=-1, keepdims=True), 1e-30)
        o_ref[...] = _dot(p.astype(BF16), vcmp_b)
        gsum = gsum_ref[...]
        for kvh in range(KVH):
            psum = p[kvh * GB * 8:kvh * GB * 8 + 8]
            for g in range(1, GB):
                psum = psum + p[(kvh * GB + g) * 8:(kvh * GB + g + 1) * 8]
            sel_ref[kvh] = _select_blocks(psum, qpos[:8], gsum, nsb)


def _cmp_sample(pt, q_heads, k_new, v_new, pool_k, pool_v, n_pages, past_len, n_new, wp):
    s = q_heads.shape[0]
    page = pool_k.shape[2]
    ppc = math.gcd(NSA_PAGES_PER_STEP, n_pages)
    tk_pad = _round_up(past_len + n_new, SEL_BLOCK)
    nc = tk_pad // CMP_STRIDE
    nsb = tk_pad // SEL_BLOCK
    m_rows = _round_up(nc, LANES)
    nsbp = _round_up(nsb, LANES)
    mc_rows = _round_up(nc, 8)
    buf_rows = CMP_STRIDE * (mc_rows - 1) + L_CMP
    gsum = ((jnp.arange(m_rows)[:, None] // (SEL_BLOCK // CMP_STRIDE) == jnp.arange(nsbp)[None, :])
            & (jnp.arange(m_rows)[:, None] < nc)).astype(BF16)
    cw = [_const_spec((L_CMP * LANES, KVH * CMP_HID)), _const_spec((1, L_CMP * LANES)), _const_spec((1, KVH * CMP_HID)),
          _const_spec((KVH * CMP_HID, LANES)), _const_spec((1, LANES))]
    per_seq = lambda shape: pl.BlockSpec((None,) + shape, lambda si, c, pt: (si,) + (0,) * len(shape))
    return pl.pallas_call(
        functools.partial(_cmp_sample_kernel, ppc=ppc, page=page, past_len=past_len, m_rows=m_rows, mc_rows=mc_rows,
                          nsb=nsb),
        out_shape=(jax.ShapeDtypeStruct((s, HB * 8, LANES), F32), jax.ShapeDtypeStruct((s, KVH, 8, nsbp), F32)),
        grid_spec=pltpu.PrefetchScalarGridSpec(
            num_scalar_prefetch=1, grid=(s, n_pages // ppc),
            in_specs=[per_seq((HB * 8, LANES)), per_seq((8, LANES)), per_seq((8, LANES))] + cw + cw
                     + [_const_spec((1, LANES)), _const_spec((256, 256)), _const_spec((m_rows, nsbp))]
                     + _page_specs(n_pages, ppc, page, LANES) + _page_specs(n_pages, ppc, page, LANES),
            out_specs=(per_seq((HB * 8, LANES)), per_seq((KVH, 8, nsbp))),
            scratch_shapes=[pltpu.VMEM((buf_rows, LANES), F32), pltpu.VMEM((buf_rows, LANES), F32)]),
        compiler_params=_cparams(("parallel", "arbitrary")),
        name="cmp_sample",
    )(pt, q_heads, k_new, v_new, *wp["cmp_k"], *wp["cmp_v"], wp["g_kc"], wp["ones"], gsum,
      *([pool_k] * ppc), *([pool_v] * ppc))


def _nsa_sample_kernel(pt_ref, q_ref, sel_ref, neg_ref, e_ref, kn_ref, vn_ref, kwin_ref, vwin_ref, kwn_ref, vwn_ref,
                       ocmp_ref, gate_ref, *rest, ppc, page, past_len, n_new, nsb, w_buf):
    k_refs, v_refs = rest[:ppc], rest[ppc:2 * ppc]
    o_ref, m_ref, l_ref, acc_ref, mw_ref, lw_ref, accw_ref = rest[2 * ppc:]
    c = pl.program_id(1)
    rows = HB * 8

    @pl.when(c == 0)
    def _():
        m_ref[...] = jnp.full(m_ref.shape, NEG_INF, F32)
        l_ref[...] = jnp.zeros_like(l_ref)
        acc_ref[...] = jnp.zeros_like(acc_ref)

    ridx = lax.broadcasted_iota(jnp.int32, (rows, 1), 0)
    qpos = past_len + ridx % 8
    slope = jnp.zeros((rows, 1), F32)
    for h in range(HB):
        slope = jnp.where(ridx // 8 == h, SLOPES_B[h], slope)
    q = q_ref[...]
    selm = jnp.concatenate([sel_ref[kvh] for kvh in range(KVH) for _ in range(GB)], axis=0)

    def bias(kpos):
        return slope * (qpos - kpos).astype(F32)

    chunk = ppc * page
    kt = jnp.concatenate([r[...] for r in k_refs], axis=1).astype(BF16)
    vt = jnp.concatenate([r[...] for r in v_refs], axis=1).astype(BF16)
    kpos = c * chunk + lax.broadcasted_iota(jnp.int32, (1, chunk), 1)
    q_aug = jnp.concatenate([q, neg_ref[...]], axis=1)
    kt_aug = jnp.concatenate([kt, e_ref[...]], axis=0)
    _online_update(m_ref, l_ref, acc_ref, _dot(q_aug, kt_aug) - bias(kpos), None, lambda p: _dot_nt(p, vt))

    @pl.when(c == pl.num_programs(1) - 1)
    def _():
        npos = past_len + lax.broadcasted_iota(jnp.int32, (1, LANES), 1)
        is_new = (npos <= qpos) & (npos < past_len + n_new)
        kn = _pad_rows(kn_ref[...], LANES).astype(BF16)
        vn = _pad_rows(vn_ref[...], LANES).astype(BF16)
        blk_new = past_len // SEL_BLOCK
        lane_s = lax.broadcasted_iota(jnp.int32, selm.shape, 1)
        picked = jnp.sum(jnp.where(lane_s == blk_new, selm, 0.0), axis=-1, keepdims=True) > 0.5
        _online_update(m_ref, l_ref, acc_ref, _dot_nt(q, kn) - bias(npos), picked & is_new, lambda p: _dot(p, vn))
        o_sel = acc_ref[...] / jnp.maximum(l_ref[...], 1e-30)
        mw_ref[...] = jnp.full(mw_ref.shape, NEG_INF, F32)
        lw_ref[...] = jnp.zeros_like(lw_ref)
        accw_ref[...] = jnp.zeros_like(accw_ref)
        kwt = kwin_ref[...].astype(BF16)
        vwt = vwin_ref[...].astype(BF16)
        wpos = past_len - w_buf + lax.broadcasted_iota(jnp.int32, (1, w_buf), 1)
        dist = qpos - wpos
        valid = (dist >= 0) & (dist < WINDOW) & (wpos >= 0)
        _online_update(mw_ref, lw_ref, accw_ref, _dot(q, kwt) - bias(wpos), valid, lambda p: _dot_nt(p, vwt))
        kwn = _pad_rows(kwn_ref[...], LANES).astype(BF16)
        vwn = _pad_rows(vwn_ref[...], LANES).astype(BF16)
        _online_update(mw_ref, lw_ref, accw_ref, _dot_nt(q, kwn) - bias(npos), is_new & (qpos - npos < WINDOW),
                       lambda p: _dot(p, vwn))
        o_win = accw_ref[...] / jnp.maximum(lw_ref[...], 1e-30)
        g = gate_ref[...]
        o_ref[...] = g[:, 0:1] * ocmp_ref[...] + g[:, 1:2] * o_sel + g[:, 2:3] * o_win


def _nsa_sample(pt, qbd, sel, k_new, v_new, kwin_t, vwin_t, kw_new, vw_new, ocmp, gates, pool_kt, pool_vt,
                n_pages, past_len, n_new, layer):
    s = qbd.shape[0]
    page = pool_kt.shape[2]
    w_buf = kwin_t.shape[2]
    ppc = math.gcd(NSA_PAGES_PER_STEP, n_pages)
    nsbp = sel.shape[-1]
    nsb = _round_up(past_len + n_new, SEL_BLOCK) // SEL_BLOCK
    rows = HB * 8
    chunk = ppc * page
    nchunk = n_pages // ppc
    bpc = chunk // SEL_BLOCK
    assert bpc <= LANES
    emat = (jnp.arange(chunk)[None, :] // SEL_BLOCK == jnp.arange(LANES)[:, None]).astype(BF16)
    neg = ((sel[..., :past_len // SEL_BLOCK] - 1.0) * MASK_NEG).reshape(s, KVH, 1, 8, nchunk, bpc)
    neg = jnp.broadcast_to(neg, (s, KVH, GB, 8, nchunk, bpc)).transpose(0, 4, 1, 2, 3, 5).reshape(s, nchunk, rows, bpc)
    neg = jnp.pad(neg, ((0, 0), (0, 0), (0, 0), (0, LANES - bpc))).astype(BF16)
    per_seq = lambda shape: pl.BlockSpec((None,) + shape, lambda si, c, pt: (si,) + (0,) * len(shape))
    state = pl.BlockSpec((None, LANES, w_buf), lambda si, c, pt: (layer * s + si, 0, 0))
    scratch = [pltpu.VMEM((rows, 1), F32), pltpu.VMEM((rows, 1), F32), pltpu.VMEM((rows, LANES), F32)]
    return pl.pallas_call(
        functools.partial(_nsa_sample_kernel, ppc=ppc, page=page, past_len=past_len, n_new=n_new, nsb=nsb, w_buf=w_buf),
        out_shape=jax.ShapeDtypeStruct((s, rows, LANES), F32),
        grid_spec=pltpu.PrefetchScalarGridSpec(
            num_scalar_prefetch=1, grid=(s, n_pages // ppc),
            in_specs=[per_seq((rows, LANES)), per_seq((KVH, 8, nsbp)),
                      pl.BlockSpec((None, None, rows, LANES), lambda si, c, pt: (si, c, 0, 0)),
                      _const_spec((LANES, chunk)),
                      per_seq((8, LANES)), per_seq((8, LANES)), state, state, per_seq((8, LANES)), per_seq((8, LANES)),
                      per_seq((rows, LANES)), per_seq((rows, LANES))]
                     + _page_specs(n_pages, ppc, LANES, page) + _page_specs(n_pages, ppc, LANES, page),
            out_specs=per_seq((rows, LANES)),
            scratch_shapes=scratch + scratch),
        compiler_params=_cparams(("parallel", "arbitrary")),
        name="nsa_sample",
    )(pt, qbd, sel, neg, emat, k_new, v_new, kwin_t, vwin_t, kw_new, vw_new, ocmp, gates,
      *([pool_kt] * ppc), *([pool_vt] * ppc))


def _prep_cmp(w1, pe, b1, w2, b2):
    eye = jnp.eye(KVH, dtype=F32)
    w1b = jnp.einsum('jde,hk->jhdke', w1, eye).reshape(L_CMP * KVH * DB, KVH * CMP_HID).astype(BF16)
    pef = jnp.tile(pe[:, None, :], (1, KVH, 1)).reshape(1, L_CMP * KVH * DB)
    w2b = jnp.einsum('ed,hk->hekd', w2, eye).reshape(KVH * CMP_HID, KVH * DB).astype(BF16)
    return (w1b, pef, jnp.tile(b1, KVH).reshape(1, -1), w2b, jnp.tile(b2, KVH).reshape(1, -1))


def _prep_layer(l, w_in, norm1_g, norm2_g, qn_a, kn_a, hn_a, qn_b, kn_cmp, kn_sel, kn_win, cmp_k, cmp_v,
                w_up_a, w_up_b, w_out, w_group, b_group, w_router, b_router, w_e_gate, w_e_up, w_e_down):
    d = w_in.shape[1]
    wi = w_in[l]
    o_qb = 2 * HA * 2 * DA + HA * DVA
    o_kv = o_qb + HB * DB
    o_gb = o_kv + 6 * KVH * DB
    o_gm = o_gb + 3 * HB
    wqb = wi[:, o_qb:o_kv].reshape(d, KVH, GB, DB)
    slot = jnp.zeros((d, KVH, GB, KVH, DB), F32)
    for kvh in range(KVH):
        slot = slot.at[:, kvh, :, kvh, :].set(wqb[:, kvh])
    wqb_ext = slot.reshape(d, HB * LANES)
    wgb = jnp.pad(wi[:, o_gb:o_gm], ((0, 0), (0, LANES - 3 * HB)))
    w_all = jnp.concatenate([wi[:, :o_qb], wqb_ext, wi[:, o_kv:o_gb], wgb, wi[:, o_gm:]], axis=1).astype(BF16)
    assert w_all.shape[1] == _W_COLS
    gq = jnp.zeros((KVH, GB, KVH, DB), F32)
    for kvh in range(KVH):
        gq = gq.at[kvh, :, kvh, :].set(jnp.broadcast_to(qn_b[l], (GB, DB)))
    ones = (jnp.arange(256)[:, None] // 64 == jnp.arange(256)[None, :] // 64).astype(BF16)
    w_r = jnp.concatenate([w_group[l], w_router[l].reshape(d, N_EXPERTS)], axis=1)
    w_r = jnp.pad(w_r, ((0, 0), (0, LANES - w_r.shape[1])))
    w_r_hi = w_r.astype(BF16)

    def by_group(w):
        w = w.reshape(N_GROUPS, EXPERTS_PER_GROUP, d, D_EXPERT).transpose(0, 2, 1, 3)
        return w.reshape(N_GROUPS, d, EXPERTS_PER_GROUP * D_EXPERT).astype(BF16)
    b_r = jnp.pad(jnp.concatenate([b_group[l], b_router[l].reshape(-1)]), (0, LANES - N_GROUPS - N_EXPERTS))
    return {
        "norm1_g": norm1_g[l].reshape(1, d), "norm2_g": norm2_g[l].reshape(1, d), "w_in": w_all,
        "g_qa": jnp.tile(qn_a[l], HA * 2).reshape(1, -1), "g_ka": jnp.tile(kn_a[l], HA * 2).reshape(1, -1),
        "g_qb": gq.reshape(1, -1), "g_ks": jnp.tile(kn_sel[l], KVH).reshape(1, -1),
        "g_kw": jnp.tile(kn_win[l], KVH).reshape(1, -1), "g_kc": jnp.tile(kn_cmp[l], KVH).reshape(1, -1),
        "hn_a": hn_a[l].reshape(1, DVA), "ones": ones,
        "cmp_k": _prep_cmp(*[a[l] for a in cmp_k]), "cmp_v": _prep_cmp(*[a[l] for a in cmp_v]),
        "w_up_a": w_up_a[l].astype(BF16), "w_up_b": w_up_b[l].astype(BF16), "w_out": w_out[l].astype(BF16),
        "w_r_hi": w_r_hi, "w_r_lo": (w_r - w_r_hi.astype(F32)).astype(BF16), "b_r": b_r.reshape(1, LANES),
        "w_gate": by_group(w_e_gate[l]), "w_up": by_group(w_e_up[l]),
        "w_down": w_e_down[l].reshape(N_GROUPS, EXPERTS_PER_GROUP * D_EXPERT, d).astype(BF16),
    }


def kernel(x_prompt, x_sample, cache_diff_k, cache_diff_v, cache_cmp_k, cache_cmp_v, cache_sel_k, cache_sel_v, state_win_k, state_win_v, page_table, c_prompt, c_sample, w_ada, b_ada, norm1_g, norm2_g, w_in, qn_a, kn_a, hn_a, lam_q1, lam_k1, lam_q2, lam_k2, qn_b, kn_cmp, kn_sel, kn_win, cmp_pe_k, cmp_w1_k, cmp_b1_k, cmp_w2_k, cmp_b2_k, cmp_pe_v, cmp_w1_v, cmp_b1_v, cmp_w2_v, cmp_b2_v, w_up_a, w_up_b, w_out, w_group, b_group, w_router, b_router, w_e_gate, w_e_up, w_e_down):
    b, t, d = x_prompt.shape
    s, n_new, _ = x_sample.shape
    depth, n_phys, page = cache_diff_k.shape[:3]
    n_pages = page_table.shape[1]
    past_len = n_pages * page
    w_buf = state_win_k.shape[2]
    assert n_new <= 8 and past_len % SEL_BLOCK == 0 and t % LANES == 0

    n_flat = depth * n_phys

    def page_t(p):
        nd = p.ndim
        return p.transpose(0, 1, *range(3, nd), 2).reshape(n_flat, -1, page)

    pool_diff_kt = page_t(cache_diff_k)
    pool_diff_v = cache_diff_v.reshape(n_flat, page * HA, DVA)
    pool_cmp_k, pool_cmp_v = page_t(cache_cmp_k), page_t(cache_cmp_v)
    pool_sel_kt, pool_sel_vt = page_t(cache_sel_k), page_t(cache_sel_v)
    win_kt = state_win_k.transpose(0, 1, 3, 4, 2).reshape(depth * s, KVH * DB, w_buf)
    win_vt = state_win_v.transpose(0, 1, 3, 4, 2).reshape(depth * s, KVH * DB, w_buf)
    slopes_a = jnp.asarray(SLOPES_A, F32)

    xp = x_prompt.reshape(b * t, d)
    xs = x_sample.reshape(s * n_new, d)
    c_all = jnp.concatenate([c_prompt, c_sample], axis=0)
    new_p = [[] for _ in range(8)]
    new_s = [[] for _ in range(8)]
    for l in range(depth):
        lambda_init = 0.8 - 0.6 * math.exp(-0.3 * l)
        out_scale = 1.0 - lambda_init
        wp = _prep_layer(l, w_in, norm1_g, norm2_g, qn_a, kn_a, hn_a, qn_b, kn_cmp, kn_sel, kn_win,
                         (cmp_w1_k, cmp_pe_k, cmp_b1_k, cmp_w2_k, cmp_b2_k),
                         (cmp_w1_v, cmp_pe_v, cmp_b1_v, cmp_w2_v, cmp_b2_v),
                         w_up_a, w_up_b, w_out, w_group, b_group, w_router, b_router, w_e_gate, w_e_up, w_e_down)
        mod, lam_row = _ada(c_all, w_ada[l], b_ada[l], lam_q1[l], lam_k1[l], lam_q2[l], lam_k2[l], lambda_init)
        lam = lam_row[0, :1]
        mods_p = [m.reshape(b, 1, d) for m in jnp.split(mod[:b], 6, axis=-1)]
        mods_s = [jnp.repeat(m, n_new, axis=0) for m in jnp.split(mod[b:], 6, axis=-1)]

        (qa, ka, va, qb, kc, vc, ks, vs, kw, vw, gbs, gm) = _proj(xp, mods_p[1], mods_p[0], False, t, wp)
        r3 = lambda a: a.reshape(b, t, -1)
        oa = _diff_prompt(r3(qa), r3(ka), r3(va), slopes_a, lam, wp["hn_a"], out_scale)
        ocmp, sel = _cmp_prompt(r3(qb), r3(kc), r3(vc), wp)
        ob = _nsa_prompt(r3(qb), r3(ks), r3(vs), r3(kw), r3(vw), sel, ocmp, r3(gbs))
        x1, h2, cw = _merge(oa.reshape(b * t, -1), ob.reshape(b * t, -1), gm, xp, mods_p[2], mods_p[4], mods_p[3],
                            False, t, wp)
        xp = _moe(h2, cw, x1, mods_p[5], False, t, wp)
        wp_rows = min(WINDOW, t)
        rows_p = [ka.reshape(b, t, HA, 2, DA), va.reshape(b, t, HA, DVA)] + \
                 [a.reshape(b, t, KVH, DB) for a in (kc, vc, ks, vs)] + \
                 [a.reshape(b, t, KVH, DB)[:, -wp_rows:] for a in (kw, vw)]

        (qa, ka, va, qb, kc, vc, ks, vs, kw, vw, gbs, gm) = _proj(xs, mods_s[1], mods_s[0], True, n_new, wp)
        pt = (page_table + l * n_phys).reshape(-1).astype(jnp.int32)
        pad8 = lambda a: jnp.pad(a.reshape(s, n_new, -1), ((0, 0), (0, 8 - n_new), (0, 0)))
        qa8 = pad8(qa)
        own = (jnp.arange(HA * 2 * DA)[None, :] // DA == jnp.arange(HA * 2)[:, None]).astype(BF16)
        qbd = (qa8[:, None, :, :] * own[None, :, None, :]).reshape(s, HA * 2 * 8, HA * 2 * DA)
        oa_s = _diff_sample(pt, lam, qbd, pad8(ka), pad8(va), wp["hn_a"], pool_diff_kt, pool_diff_v,
                            n_pages, past_len, n_new, out_scale)
        oa_s = oa_s[:, :n_new].reshape(s * n_new, HA * DVA)
        q_heads = pad8(qb).reshape(s, 8, HB, LANES).transpose(0, 2, 1, 3)
        ocmp_s, sel_s = _cmp_sample(pt, q_heads.reshape(s, HB * 8, LANES), pad8(kc), pad8(vc), pool_cmp_k, pool_cmp_v, n_pages, past_len, n_new, wp)
        kwin = jnp.concatenate([state_win_k[l].reshape(s, w_buf, -1), kw.reshape(s, n_new, -1)], axis=1)
        vwin = jnp.concatenate([state_win_v[l].reshape(s, w_buf, -1), vw.reshape(s, n_new, -1)], axis=1)
        gates = pad8(gbs)[:, :, :3 * HB].reshape(s, 8, HB, 3).transpose(0, 2, 1, 3).reshape(s, HB * 8, 3)
        gates = jnp.pad(gates, ((0, 0), (0, 0), (0, LANES - 3)))
        ob_s = _nsa_sample(pt, q_heads.reshape(s, HB * 8, LANES), sel_s, pad8(ks), pad8(vs), win_kt, win_vt,
                           pad8(kw), pad8(vw), ocmp_s.reshape(s, HB * 8, LANES), gates, pool_sel_kt, pool_sel_vt,
                           n_pages, past_len, n_new, l)
        ob_s = ob_s.reshape(s, KVH, GB, 8, KVH, DB)
        ob_s = jnp.stack([ob_s[:, kvh, :, :, kvh, :] for kvh in range(KVH)], axis=1)
        ob_s = ob_s.transpose(0, 3, 1, 2, 4)[:, :n_new].reshape(s * n_new, HB * DB).astype(BF16)
        x1, h2, cw = _merge(oa_s, ob_s, gm, xs, mods_s[2], mods_s[4], mods_s[3], True, n_new, wp)
        xs = _moe(h2, cw, x1, mods_s[5], True, n_new, wp)
        rows_s = [ka.reshape(s, n_new, HA, 2, DA), va.reshape(s, n_new, HA, DVA)] + \
                 [a.reshape(s, n_new, KVH, DB) for a in (kc, vc, ks, vs)] + \
                 [kwin[:, -w_buf:].reshape(s, w_buf, KVH, DB), vwin[:, -w_buf:].reshape(s, w_buf, KVH, DB)]
        for i in range(8):
            new_p[i].append(rows_p[i])
            new_s[i].append(rows_s[i])

    outs_p = [jnp.stack(a) for a in new_p]
    outs_s = [jnp.stack(a) for a in new_s]
    return (xp.reshape(b, t, d), xs.reshape(s, n_new, d), *outs_p, *outs_s)
```

```python
import functools
import math

import jax
import jax.numpy as jnp
from jax import lax
from jax.experimental import pallas as pl
from jax.experimental.pallas import tpu as pltpu

F32 = jnp.float32
BF16 = jnp.bfloat16

HA, DA, DVA = 4, 64, 128
HB, KVH, GB, DB = 8, 2, 4, 64
L_CMP, CMP_STRIDE, CMP_HID = 32, 16, 128
SEL_BLOCK, N_SEL, WINDOW = 64, 8, 512
N_GROUPS, EXPERTS_PER_GROUP, N_EXPERTS, D_EXPERT = 4, 4, 16, 256
RMS_EPS = 1e-6
NEG_INF = -1e30
BIG = 1e9
REMOVED = -3e38
MASK_NEG = 1e30
AUG_POS = DB - 2
SLOPES_A = tuple(2.0 ** (-8.0 * (i + 1) / HA) for i in range(HA))
SLOPES_B = tuple(2.0 ** (-8.0 * (i + 1) / HB) for i in range(HB))
QK_SCALE = 0.125

LANES = 128
VMEM_LIMIT = 56 * 1024 * 1024

ROW_TILE = 256
MOE_ROW_TILE = 512
DIFF_TQ = 256
NSA_TQ = 128
NSA_TK = 256
CMP_TQ = 256
DIFF_PAGES_PER_STEP = 16
NSA_PAGES_PER_STEP = 32


def _cparams(sem):
    return pltpu.CompilerParams(dimension_semantics=sem, vmem_limit_bytes=VMEM_LIMIT)


def _const_spec(shape):
    nd = len(shape)
    return pl.BlockSpec(shape, lambda *_: (0,) * nd)


def _round_up(x, m):
    return -(-x // m) * m


def _split3(x):
    x1 = x.astype(BF16)
    r = x - x1.astype(F32)
    x2 = r.astype(BF16)
    x3 = (r - x2.astype(F32)).astype(BF16)
    return x1, x2, x3


def _dot(a, b):
    return jnp.dot(a, b, preferred_element_type=F32)


def _dot_nt(a, b):
    return lax.dot_general(a, b, (((1,), (1,)), ((), ())), preferred_element_type=F32)


def _gelu_tanh(x):
    return 0.5 * x * (1.0 + jnp.tanh(0.7978845608028654 * (x + 0.044715 * x * x * x)))


def _topk_mask(imp, k):
    width = imp.shape[-1]
    lane = lax.broadcasted_iota(jnp.int32, imp.shape, 1).astype(F32)
    sel = jnp.zeros(imp.shape, F32)
    work = imp
    for _ in range(k):
        m = jnp.max(work, axis=-1, keepdims=True)
        idx = jnp.min(jnp.where(work == m, lane, float(width)), axis=-1, keepdims=True)
        hit = lane == idx
        sel = jnp.where(hit, 1.0, sel)
        work = jnp.where(hit, REMOVED, work)
    return sel


def _ada_kernel(c_ref, w_ref, b_ref, q1_ref, k1_ref, q2_ref, k2_ref, o_ref, lam_ref, *, lambda_init):
    c = c_ref[...]
    s = c * jax.nn.sigmoid(c)
    s1, s2, s3 = _split3(s)
    w1, w2, w3 = _split3(w_ref[...])
    acc = _dot(s1, w1) + (_dot(s1, w2) + _dot(s2, w1)) + (_dot(s1, w3) + _dot(s2, w2) + _dot(s3, w1))
    o_ref[...] = acc + b_ref[...]
    a = jnp.sum(q1_ref[...] * k1_ref[...], axis=-1, keepdims=True)
    b = jnp.sum(q2_ref[...] * k2_ref[...], axis=-1, keepdims=True)
    lam_ref[...] = jnp.broadcast_to(jnp.exp(a) - jnp.exp(b) + lambda_init, lam_ref.shape)


def _ada(c_all, w_ada, b_ada, lq1, lk1, lq2, lk2, lambda_init):
    n, d = c_all.shape
    n_out = w_ada.shape[1]
    tn = 512
    vec = pl.BlockSpec((1, DA), lambda j: (0, 0))
    return pl.pallas_call(
        functools.partial(_ada_kernel, lambda_init=lambda_init),
        out_shape=(jax.ShapeDtypeStruct((n, n_out), F32), jax.ShapeDtypeStruct((1, LANES), F32)),
        grid=(n_out // tn,),
        in_specs=[pl.BlockSpec((n, d), lambda j: (0, 0)), pl.BlockSpec((d, tn), lambda j: (0, j)),
                  pl.BlockSpec((1, tn), lambda j: (0, j)), vec, vec, vec, vec],
        out_specs=(pl.BlockSpec((n, tn), lambda j: (0, j)), pl.BlockSpec((1, LANES), lambda j: (0, 0))),
        compiler_params=_cparams(("arbitrary",)),
        name="ada",
    )(c_all, w_ada, b_ada.reshape(1, -1), lq1.reshape(1, -1), lk1.reshape(1, -1), lq2.reshape(1, -1), lk2.reshape(1, -1))


_W_QA, _W_KA, _W_VA, _W_QB = 0, 512, 1024, 1536
_W_KV6 = _W_QB + HB * LANES
_W_GB = _W_KV6 + 6 * 128
_W_GM = _W_GB + 128
_W_COLS = _W_GM + 2048


def _group_norm(p, gain, ones):
    n = p.shape[1]
    w = 256 if n % 256 == 0 else 128
    sq = (p * p).astype(BF16)
    parts = [_dot(sq[:, c:c + w], ones[:w, :w]) for c in range(0, n, w)]
    ss = parts[0] if len(parts) == 1 else jnp.concatenate(parts, axis=1)
    return p * lax.rsqrt(ss * (1.0 / 64.0) + RMS_EPS) * gain


def _proj_kernel(x_ref, sc_ref, sh_ref, ng_ref, w_ref, gqa_ref, gka_ref, gqb_ref, gks_ref, gkw_ref, ones_ref,
                 qa_ref, ka_ref, va_ref, qb_ref, kc_ref, vc_ref, ks_ref, vs_ref, kw_ref, vw_ref, gb_ref, gm_ref):
    x = x_ref[...]
    h = x * lax.rsqrt(jnp.mean(x * x, axis=-1, keepdims=True) + RMS_EPS) * ng_ref[...]
    h = h * (1.0 + sc_ref[...]) + sh_ref[...]
    hb = h.astype(BF16)
    ones = ones_ref[...]

    def seg(a, n):
        return _dot(hb, w_ref[:, a:a + n])

    qa_ref[...] = (_group_norm(seg(_W_QA, 512), gqa_ref[...], ones) * QK_SCALE).astype(BF16)
    ka_ref[...] = _group_norm(seg(_W_KA, 512), gka_ref[...], ones)
    va_ref[...] = seg(_W_VA, 512)
    qb_ref[...] = (_group_norm(seg(_W_QB, 1024), gqb_ref[...], ones) * QK_SCALE).astype(BF16)
    kc_ref[...] = seg(_W_KV6, 128)
    vc_ref[...] = seg(_W_KV6 + 128, 128)
    ks_ref[...] = _group_norm(seg(_W_KV6 + 256, 128), gks_ref[...], ones)
    vs_ref[...] = seg(_W_KV6 + 384, 128)
    kw_ref[...] = _group_norm(seg(_W_KV6 + 512, 128), gkw_ref[...], ones)
    vw_ref[...] = seg(_W_KV6 + 640, 128)
    gb_ref[...] = jax.nn.sigmoid(seg(_W_GB, 128))
    gm_ref[:, :1024] = jax.nn.sigmoid(seg(_W_GM, 1024)).astype(BF16)
    gm_ref[:, 1024:] = jax.nn.sigmoid(seg(_W_GM + 1024, 1024)).astype(BF16)


def _mod_spec(per_token, tm, bps, d):
    if per_token:
        return pl.BlockSpec((tm, d), lambda i, *_: (i, 0))
    return pl.BlockSpec((None, 1, d), lambda i, *_: (i // bps, 0, 0))


def _proj(x, sc, sh, per_token, rows_per_seq, wp):
    n, d = x.shape
    tm = min(ROW_TILE, n)
    bps = max(rows_per_seq // tm, 1)
    mod = _mod_spec(per_token, tm, bps, d)

    def row(width):
        return pl.BlockSpec((tm, width), lambda i: (i, 0))

    widths = [(512, BF16), (512, F32), (512, F32), (1024, BF16)] + [(128, F32)] * 6 + [(128, F32), (2048, BF16)]
    return pl.pallas_call(
        _proj_kernel,
        out_shape=tuple(jax.ShapeDtypeStruct((n, w), dt) for w, dt in widths),
        grid=(n // tm,),
        in_specs=[row(d), mod, mod, _const_spec((1, d)), _const_spec((d, _W_COLS)),
                  _const_spec((1, 512)), _const_spec((1, 512)), _const_spec((1, 1024)),
                  _const_spec((1, 128)), _const_spec((1, 128)), _const_spec((256, 256))],
        out_specs=tuple(row(w) for w, _ in widths),
        compiler_params=_cparams(("parallel",)),
        name="proj",
    )(x, sc, sh, wp["norm1_g"], wp["w_in"], wp["g_qa"], wp["g_ka"], wp["g_qb"], wp["g_ks"], wp["g_kw"], wp["ones"])


POS_RADIX = 256


def _pos_lanes(n_keys):
    k = jnp.arange(n_keys)
    return jnp.stack([k // POS_RADIX, k % POS_RADIX], axis=1).astype(F32)


def _flash_step(m, acc, s, valid, v_ext):
    if valid is not None:
        s = jnp.where(valid, s, NEG_INF)
    m_new = jnp.maximum(m, jnp.max(s, axis=-1, keepdims=True))
    alpha = jnp.exp(m - m_new)
    p = jnp.exp(s - m_new)
    acc = alpha * acc + _dot(p.astype(BF16), v_ext)
    return m_new, acc


DIFF_HEADS_PER_STEP = 4


def _diff_prompt_kernel(slope_ref, lam_ref, q_ref, k_ref, v_ref, pos_ref, hn_ref, o_ref, ka0, ka1, vext,
                        *, tq, out_scale):
    hp = pl.program_id(1)
    i = pl.program_id(2)
    lam = lam_ref[0]
    nh = DIFF_HEADS_PER_STEP

    @pl.when(i == 0)
    def _():
        pos = pos_ref[...]
        lane = lax.broadcasted_iota(jnp.int32, pos.shape, 1)
        for hh in range(nh):
            k = k_ref[:, hh * 2 * DA:(hh + 1) * 2 * DA]
            ka0[hh] = jnp.where(lane < DA, k, pos).astype(BF16)
            ka1[hh] = jnp.where(lane >= DA, k, pos).astype(BF16)
            vext[hh, :, :DVA] = v_ref[:, hh * DVA:(hh + 1) * DVA].astype(BF16)
            vext[hh, :, DVA:] = jnp.ones((vext.shape[1], DVA), BF16)

    lane = lax.broadcasted_iota(jnp.int32, (tq, 2 * DA), 1)
    half = lane % DA
    chains = []
    for hh in range(nh):
        slope = slope_ref[hp * nh + hh]
        q = q_ref[:, hh * 2 * DA:(hh + 1) * 2 * DA].astype(F32)
        digits = jnp.where(half == 0, slope * POS_RADIX, jnp.where(half == 1, slope, 0.0))
        chains.append((hh, ka0, jnp.where(lane < DA, q, digits).astype(BF16)))
        chains.append((hh, ka1, jnp.where(lane >= DA, q, digits).astype(BF16)))
    r = lax.broadcasted_iota(jnp.int32, (tq, tq), 0)
    c = lax.broadcasted_iota(jnp.int32, (tq, tq), 1)
    causal = r >= c

    def block(j, carry, valid):
        start = pl.multiple_of(j * tq, tq)
        out = []
        for (hh, kaug, qm), (m, acc) in zip(chains, carry):
            s = _dot_nt(qm, kaug[hh, pl.ds(start, tq), :])
            out.append(_flash_step(m, acc, s, valid, vext[hh, pl.ds(start, tq), :]))
        return tuple(out)

    init = tuple((jnp.full((tq, 1), NEG_INF, F32), jnp.zeros((tq, 2 * DVA), F32)) for _ in chains)
    carry = lax.fori_loop(0, i, lambda j, cr: block(j, cr, None), init)
    carry = block(i, carry, causal)
    outs = [acc[:, :DVA] / acc[:, DVA:] for _, acc in carry]
    for hh in range(nh):
        o = outs[2 * hh] - lam * outs[2 * hh + 1]
        o = o * lax.rsqrt(jnp.mean(o * o, axis=-1, keepdims=True) + RMS_EPS) * hn_ref[...]
        o_ref[:, hh * DVA:(hh + 1) * DVA] = (o * out_scale).astype(BF16)


def _diff_prompt(qa, ka, va, slopes, lam, hn, out_scale):
    b, t, _ = qa.shape
    tq = min(DIFF_TQ, t)
    nh = DIFF_HEADS_PER_STEP
    assert t <= POS_RADIX * POS_RADIX and HA % nh == 0
    smem = pl.BlockSpec(memory_space=pltpu.SMEM)
    pos = jnp.pad(_pos_lanes(t), ((0, 0), (0, DA - 2)))
    pos = jnp.concatenate([pos, pos], axis=1)
    return pl.pallas_call(
        functools.partial(_diff_prompt_kernel, tq=tq, out_scale=out_scale),
        out_shape=jax.ShapeDtypeStruct((b, t, HA * DVA), BF16),
        grid=(b, HA // nh, t // tq),
        in_specs=[smem, smem,
                  pl.BlockSpec((None, tq, nh * 2 * DA), lambda bi, h, i: (bi, i, h)),
                  pl.BlockSpec((None, t, nh * 2 * DA), lambda bi, h, i: (bi, 0, h)),
                  pl.BlockSpec((None, t, nh * DVA), lambda bi, h, i: (bi, 0, h)),
                  pl.BlockSpec((t, 2 * DA), lambda bi, h, i: (0, 0)),
                  pl.BlockSpec((1, DVA), lambda bi, h, i: (0, 0))],
        out_specs=pl.BlockSpec((None, tq, nh * DVA), lambda bi, h, i: (bi, i, h)),
        scratch_shapes=[pltpu.VMEM((nh, t, 2 * DA), BF16), pltpu.VMEM((nh, t, 2 * DA), BF16),
                        pltpu.VMEM((nh, t, 2 * DVA), BF16)],
        compiler_params=_cparams(("parallel", "parallel", "arbitrary")),
        name="diff_prompt",
    )(slopes, lam, qa, ka, va, pos, hn)


def _compress(buf_ref, m_rows, w1_ref, pe_ref, b1_ref, w2_ref, b2_ref):
    acc = jnp.zeros((m_rows, KVH * CMP_HID), F32)
    group = 8
    for jg in range(L_CMP // group):
        cols = slice(jg * group * LANES, (jg + 1) * group * LANES)
        xs = [buf_ref[pl.ds(jg * group + jj, m_rows, stride=CMP_STRIDE), :] for jj in range(group)]
        x = jnp.concatenate(xs, axis=1) + pe_ref[:, cols]
        acc = acc + _dot(x.astype(BF16), w1_ref[cols, :])
    hid = _gelu_tanh(acc + b1_ref[...])
    return _dot(hid.astype(BF16), w2_ref[...]) + b2_ref[...]


def _cmp_attention(q_ext, kcmp_b, vcmp_b, qpos, slopes_h, gsum, nsb, n_q_rows):
    m_rows = kcmp_b.shape[0]
    cend = (lax.broadcasted_iota(jnp.int32, (1, m_rows), 1) * CMP_STRIDE + (L_CMP - 1))
    visible = cend <= qpos
    dist = (qpos - cend).astype(F32)
    outs = []
    psum = jnp.zeros((n_q_rows, m_rows), F32)
    for g in range(GB):
        s = _dot_nt(q_ext[g], kcmp_b) - slopes_h[g] * dist
        s = jnp.where(visible, s, NEG_INF)
        mx = jnp.max(s, axis=-1, keepdims=True)
        e = jnp.where(visible, jnp.exp(s - mx), 0.0)
        p = e / jnp.maximum(jnp.sum(e, axis=-1, keepdims=True), 1e-30)
        outs.append(_dot(p.astype(BF16), vcmp_b))
        psum = psum + p
    return outs, _select_blocks(psum, qpos, gsum, nsb)


def _select_blocks(psum, qpos, gsum, nsb):
    p1, p2, p3 = _split3(psum)
    imp = _dot(p1, gsum) + _dot(p2, gsum) + _dot(p3, gsum)
    blk = lax.broadcasted_iota(jnp.int32, (1, imp.shape[1]), 1)
    cur = qpos // SEL_BLOCK
    forced = (blk == 0) | (blk == cur) | (blk == cur - 1)
    imp = jnp.where(blk > cur, -BIG, imp)
    imp = jnp.where(forced, BIG, imp)
    imp = jnp.where(blk >= nsb, REMOVED, imp)
    return _topk_mask(imp, min(N_SEL, nsb))


def _cmp_prompt_kernel(q_ref, kc_ref, vc_ref, w1k_ref, pek_ref, b1k_ref, w2k_ref, b2k_ref,
                       w1v_ref, pev_ref, b1v_ref, w2v_ref, b2v_ref, kn_ref, ones_ref, gsum_ref,
                       o_ref, sel_ref, bufk, bufv, kcmp, vcmp, *, t, tq, m_rows, nsb):
    i = pl.program_id(1)

    @pl.when(i == 0)
    def _():
        for buf, src in ((bufk, kc_ref), (bufv, vc_ref)):
            buf[pl.ds(0, t), :] = src[...]
            buf[pl.ds(t, buf.shape[0] - t), :] = jnp.zeros((buf.shape[0] - t, LANES), F32)
        k = _compress(bufk, m_rows, w1k_ref, pek_ref, b1k_ref, w2k_ref, b2k_ref)
        kcmp[...] = _group_norm(k, kn_ref[...], ones_ref[...]).astype(BF16)
        vcmp[...] = _compress(bufv, m_rows, w1v_ref, pev_ref, b1v_ref, w2v_ref, b2v_ref).astype(BF16)

    qpos = i * tq + lax.broadcasted_iota(jnp.int32, (tq, 1), 0)
    kcmp_b = kcmp[...]
    vcmp_b = vcmp[...]
    gsum = gsum_ref[...]
    for kvh in range(KVH):
        q_ext = [q_ref[:, (kvh * GB + g) * LANES:(kvh * GB + g + 1) * LANES] for g in range(GB)]
        outs, sel = _cmp_attention(q_ext, kcmp_b, vcmp_b, qpos, SLOPES_B[kvh * GB:(kvh + 1) * GB], gsum, nsb, tq)
        for g in range(GB):
            o_ref[:, (kvh * GB + g) * LANES:(kvh * GB + g + 1) * LANES] = outs[g].astype(BF16)
        blk = lax.broadcasted_iota(jnp.int32, sel.shape, 1)
        neg = jnp.where(blk < nsb, (sel - 1.0) * MASK_NEG, 0.0)
        if kvh == 0:
            neg = pltpu.roll(neg, DB, 1)
        sel_ref[kvh] = neg.astype(BF16)


def _cmp_prompt(qb_ext, kc, vc, wp):
    b, t, _ = qb_ext.shape
    tq = min(CMP_TQ, t)
    nc = t // CMP_STRIDE
    m_rows = _round_up(nc, LANES)
    nsb = t // SEL_BLOCK
    nsbp = _round_up(nsb, LANES)
    assert nsbp == LANES and nsb <= AUG_POS
    buf_rows = CMP_STRIDE * (m_rows - 1) + L_CMP
    gsum = ((jnp.arange(m_rows)[:, None] // (SEL_BLOCK // CMP_STRIDE) == jnp.arange(nsbp)[None, :])
            & (jnp.arange(m_rows)[:, None] < nc)).astype(BF16)
    kv = pl.BlockSpec((None, t, LANES), lambda bi, i: (bi, 0, 0))
    cw = [_const_spec((L_CMP * LANES, KVH * CMP_HID)), _const_spec((1, L_CMP * LANES)), _const_spec((1, KVH * CMP_HID)),
          _const_spec((KVH * CMP_HID, LANES)), _const_spec((1, LANES))]
    return pl.pallas_call(
        functools.partial(_cmp_prompt_kernel, t=t, tq=tq, m_rows=m_rows, nsb=nsb),
        out_shape=(jax.ShapeDtypeStruct((b, t, HB * LANES), BF16), jax.ShapeDtypeStruct((b, KVH, t, nsbp), BF16)),
        grid=(b, t // tq),
        in_specs=[pl.BlockSpec((None, tq, HB * LANES), lambda bi, i: (bi, i, 0)), kv, kv] + cw + cw
                 + [_const_spec((1, LANES)), _const_spec((256, 256)), _const_spec((m_rows, nsbp))],
        out_specs=(pl.BlockSpec((None, tq, HB * LANES), lambda bi, i: (bi, i, 0)),
                   pl.BlockSpec((None, KVH, tq, nsbp), lambda bi, i: (bi, 0, i, 0))),
        scratch_shapes=[pltpu.VMEM((buf_rows, LANES), F32), pltpu.VMEM((buf_rows, LANES), F32),
                        pltpu.VMEM((m_rows, LANES), BF16), pltpu.VMEM((m_rows, LANES), BF16)],
        compiler_params=_cparams(("parallel", "arbitrary")),
        name="cmp_prompt",
    )(qb_ext, kc, vc, *wp["cmp_k"], *wp["cmp_v"], wp["g_kc"], wp["ones"], gsum)


def _compact_pairs(ext, kvh):
    lane = lax.broadcasted_iota(jnp.int32, ext[0].shape, 1)
    pieces = []
    for pair in range(GB // 2):
        even, odd = ext[2 * pair], ext[2 * pair + 1]
        if kvh == 0:
            pieces.append(jnp.where(lane < DB, even, pltpu.roll(odd, DB, 1)))
        else:
            pieces.append(jnp.where(lane < DB, pltpu.roll(even, DB, 1), odd))
    return jnp.concatenate(pieces, axis=1)


def _nsa_prompt_kernel(q_ref, ks_ref, vs_ref, kw_ref, vw_ref, sel_ref, ocmp_ref, gb_ref, aug_ref, o_ref,
                       ka_s, va_s, ka_w, va_w, *, tq, tk):
    i = pl.program_id(1)
    rows = 2 * tq

    @pl.when(i == 0)
    def _():
        lane = lax.broadcasted_iota(jnp.int32, ks_ref.shape, 1)
        for kvh in range(KVH):
            own = (lane >= kvh * DB) & (lane < (kvh + 1) * DB)
            aug = aug_ref[kvh]
            ka_s[kvh] = jnp.where(own, ks_ref[...], aug).astype(BF16)
            ka_w[kvh] = jnp.where(own, kw_ref[...], aug).astype(BF16)
            va_s[kvh] = jnp.where(own, vs_ref[...], 1.0).astype(BF16)
            va_w[kvh] = jnp.where(own, vw_ref[...], 1.0).astype(BF16)

    r = lax.broadcasted_iota(jnp.int32, (rows, tk), 0) % tq
    c = lax.broadcasted_iota(jnp.int32, (rows, tk), 1)
    dbase = r - c
    qlane = lax.broadcasted_iota(jnp.int32, (tq, LANES), 1)
    gates = gb_ref[...]
    j_hi = (i * tq + tq - 1) // tk
    j_lo_win = jnp.maximum(i * tq - (WINDOW - 1), 0) // tk

    chains = []
    for kvh in range(KVH):
        base = (1 - kvh) * DB
        own_q = (qlane >= kvh * DB) & (qlane < (kvh + 1) * DB)
        negmask = sel_ref[kvh].astype(F32)
        q_sel, q_win = [], []
        for g in range(GB):
            head = kvh * GB + g
            slot = q_ref[:, head * LANES:(head + 1) * LANES].astype(F32)
            digits = jnp.where(qlane == base + AUG_POS, SLOPES_B[head] * POS_RADIX,
                               jnp.where(qlane == base + AUG_POS + 1, SLOPES_B[head], 0.0))
            q_win.append(jnp.where(own_q, slot, digits).astype(BF16))
            q_sel.append(jnp.where(own_q, slot, digits + negmask).astype(BF16))
        for g in range(0, GB, 2):
            chains.append((kvh, g, jnp.concatenate(q_sel[g:g + 2], axis=0), jnp.concatenate(q_win[g:g + 2], axis=0)))

    def sel_block(j, carry, valid=None):
        start = pl.multiple_of(j * tk, tk)
        out = []
        for (kvh, _, qs, _), (m, acc) in zip(chains, carry):
            s = _dot_nt(qs, ka_s[kvh, pl.ds(start, tk), :])
            out.append(_flash_step(m, acc, s, valid, va_s[kvh, pl.ds(start, tk), :]))
        return tuple(out)

    span = WINDOW // tk + 1
    j0 = jnp.maximum(j_hi - (span - 1), 0)
    wstart = pl.multiple_of(j0 * tk, tk)
    rw = lax.broadcasted_iota(jnp.int32, (rows, span * tk), 0) % tq
    cw_ = lax.broadcasted_iota(jnp.int32, (rows, span * tk), 1)
    wdist = rw - cw_ - (j0 * tk - i * tq)
    wvalid = (wdist >= 0) & (wdist < WINDOW)

    def win_once():
        out = []
        for (kvh, _, _, qw) in chains:
            s = _dot_nt(qw, ka_w[kvh, pl.ds(wstart, span * tk), :])
            s = jnp.where(wvalid, s, NEG_INF)
            p = jnp.exp(s - jnp.max(s, axis=-1, keepdims=True))
            out.append((None, _dot(p.astype(BF16), va_w[kvh, pl.ds(wstart, span * tk), :])))
        return tuple(out)

    init = tuple((jnp.full((rows, 1), NEG_INF, F32), jnp.zeros((rows, LANES), F32)) for _ in chains)
    carry = lax.fori_loop(0, j_hi, sel_block, init)
    res_s = sel_block(j_hi, carry, dbase >= j_hi * tk - i * tq)
    res_w = win_once()

    def finish(acc):
        den = pltpu.roll(acc, DB, 1)
        return acc / jnp.maximum(den, 1e-30)

    ext = [[None] * GB for _ in range(KVH)]
    for (kvh, g0, _, _), (_, acc_s), (_, acc_w) in zip(chains, res_s, res_w):
        o_s, o_w = finish(acc_s), finish(acc_w)
        for gi in range(2):
            g = g0 + gi
            head = kvh * GB + g
            col = head * 3
            oc = ocmp_ref[:, head * LANES:(head + 1) * LANES].astype(F32)
            ext[kvh][g] = (gates[:, col:col + 1] * oc + gates[:, col + 1:col + 2] * o_s[gi * tq:(gi + 1) * tq]
                           + gates[:, col + 2:col + 3] * o_w[gi * tq:(gi + 1) * tq])
    for kvh in range(KVH):
        o_ref[:, kvh * GB * DB:(kvh + 1) * GB * DB] = _compact_pairs(ext[kvh], kvh).astype(BF16)


def _nsa_prompt(qb_ext, ks, vs, kw, vw, sel, ocmp, gbs):
    b, t, _ = qb_ext.shape
    tq = min(NSA_TQ, t)
    tk = min(NSA_TK, t)
    nsb = t // SEL_BLOCK
    assert tk % tq == 0 and nsb <= AUG_POS and t <= POS_RADIX * POS_RADIX
    half = jnp.concatenate([(jnp.arange(t)[:, None] // SEL_BLOCK == jnp.arange(AUG_POS)[None, :]).astype(F32),
                            _pos_lanes(t)], axis=1)
    zeros = jnp.zeros((t, DB), F32)
    aug = jnp.stack([jnp.concatenate([zeros, half], axis=1), jnp.concatenate([half, zeros], axis=1)])
    kv = pl.BlockSpec((None, t, LANES), lambda bi, i: (bi, 0, 0))
    scratch = pltpu.VMEM((KVH, t, LANES), BF16)
    return pl.pallas_call(
        functools.partial(_nsa_prompt_kernel, tq=tq, tk=tk),
        out_shape=jax.ShapeDtypeStruct((b, t, HB * DB), BF16),
        grid=(b, t // tq),
        in_specs=[pl.BlockSpec((None, tq, HB * LANES), lambda bi, i: (bi, i, 0)), kv, kv, kv, kv,
                  pl.BlockSpec((None, KVH, tq, LANES), lambda bi, i: (bi, 0, i, 0)),
                  pl.BlockSpec((None, tq, HB * LANES), lambda bi, i: (bi, i, 0)),
                  pl.BlockSpec((None, tq, LANES), lambda bi, i: (bi, i, 0)),
                  _const_spec((KVH, t, LANES))],
        out_specs=pl.BlockSpec((None, tq, HB * DB), lambda bi, i: (bi, i, 0)),
        scratch_shapes=[scratch, scratch, scratch, scratch],
        compiler_params=_cparams(("parallel", "arbitrary")),
        name="nsa_prompt",
    )(qb_ext, ks, vs, kw, vw, sel, ocmp, gbs, aug)


def _merge_kernel(oa_ref, ob_ref, gm_ref, x_ref, g1_ref, sc_ref, sh_ref, ng_ref, wa_ref, wb_ref, wo_ref,
                  wrh_ref, wrl_ref, br_ref, x1_ref, h2_ref, cw_ref):
    ya = _dot(oa_ref[...], wa_ref[...])
    yb = _dot(ob_ref[...], wb_ref[...])
    d = ya.shape[1]
    t = gm_ref[:, :d].astype(F32) * ya + gm_ref[:, d:].astype(F32) * yb
    mix = _dot(t.astype(BF16), wo_ref[...])
    x1 = x_ref[...] + g1_ref[...] * mix
    x1_ref[...] = x1
    h2 = x1 * lax.rsqrt(jnp.mean(x1 * x1, axis=-1, keepdims=True) + RMS_EPS) * ng_ref[...]
    h2 = h2 * (1.0 + sc_ref[...]) + sh_ref[...]
    h2_ref[...] = h2.astype(BF16)
    hh = h2.astype(BF16)
    hl = (h2 - hh.astype(F32)).astype(BF16)
    logits = _dot(hh, wrh_ref[...]) + (_dot(hh, wrl_ref[...]) + _dot(hl, wrh_ref[...])) + br_ref[...]
    lane = lax.broadcasted_iota(jnp.int32, logits.shape, 1)
    lane_f = lane.astype(F32)
    is_grp = lane < N_GROUPS
    lg = jnp.where(is_grp, logits, NEG_INF)
    gmax = jnp.max(lg, axis=-1, keepdims=True)
    gidx = jnp.min(jnp.where(lg == gmax, lane_f, float(LANES)), axis=-1, keepdims=True)
    p_grp = 1.0 / jnp.sum(jnp.where(is_grp, jnp.exp(lg - gmax), 0.0), axis=-1, keepdims=True)
    egrp = ((lane - N_GROUPS) // EXPERTS_PER_GROUP).astype(F32)
    in_grp = (lane >= N_GROUPS) & (lane < N_GROUPS + N_EXPERTS) & (egrp == gidx)
    le = jnp.where(in_grp, logits, REMOVED)
    v1 = jnp.max(le, axis=-1, keepdims=True)
    i1 = jnp.min(jnp.where(le == v1, lane_f, float(LANES)), axis=-1, keepdims=True)
    le2 = jnp.where(lane_f == i1, REMOVED, le)
    v2 = jnp.max(le2, axis=-1, keepdims=True)
    i2 = jnp.min(jnp.where(le2 == v2, lane_f, float(LANES)), axis=-1, keepdims=True)
    e2 = jnp.exp(v2 - v1)
    w1 = 1.0 / (1.0 + e2)
    cw_ref[...] = jnp.where(lane_f == i1, w1 * p_grp, jnp.where(lane_f == i2, e2 * w1 * p_grp, 0.0))


def _merge(oa, ob, gm, x, g1, sc2, sh2, per_token, rows_per_seq, wp):
    n, d = x.shape
    tm = min(ROW_TILE, n)
    bps = max(rows_per_seq // tm, 1)
    mod = _mod_spec(per_token, tm, bps, d)

    def row(width):
        return pl.BlockSpec((tm, width), lambda i: (i, 0))

    return pl.pallas_call(
        _merge_kernel,
        out_shape=(jax.ShapeDtypeStruct((n, d), F32), jax.ShapeDtypeStruct((n, d), BF16),
                   jax.ShapeDtypeStruct((n, LANES), F32)),
        grid=(n // tm,),
        in_specs=[row(HA * DVA), row(HB * DB), row(2 * d), row(d), mod, mod, mod, _const_spec((1, d)),
                  _const_spec((HA * DVA, d)), _const_spec((HB * DB, d)), _const_spec((d, d)),
                  _const_spec((d, LANES)), _const_spec((d, LANES)), _const_spec((1, LANES))],
        out_specs=(row(d), row(d), row(LANES)),
        compiler_params=_cparams(("parallel",)),
        name="merge",
    )(oa, ob, gm, x, g1, sc2, sh2, wp["norm2_g"], wp["w_up_a"], wp["w_up_b"], wp["w_out"],
      wp["w_r_hi"], wp["w_r_lo"], wp["b_r"])


def _moe_kernel(h_ref, cw_ref, x_ref, g2_ref, wg_ref, wu_ref, wd_ref, o_ref, acc_ref):
    grp = pl.program_id(1)

    @pl.when(grp == 0)
    def _():
        acc_ref[...] = jnp.zeros_like(acc_ref)

    lane = lax.broadcasted_iota(jnp.int32, cw_ref.shape, 1)
    cw_all = cw_ref[...]
    h = h_ref[...]
    a = _dot(h, wg_ref[...])
    u = _dot(h, wu_ref[...])
    cols = []
    for el in range(EXPERTS_PER_GROUP):
        cw = jnp.sum(jnp.where(lane == N_GROUPS + grp * EXPERTS_PER_GROUP + el, cw_all, 0.0), axis=-1, keepdims=True)
        ae = a[:, el * D_EXPERT:(el + 1) * D_EXPERT]
        cols.append((ae * jax.nn.sigmoid(ae) * u[:, el * D_EXPERT:(el + 1) * D_EXPERT] * cw).astype(BF16))
    acc_ref[...] += _dot(jnp.concatenate(cols, axis=1), wd_ref[...])

    @pl.when(grp == pl.num_programs(1) - 1)
    def _():
        o_ref[...] = x_ref[...] + g2_ref[...] * acc_ref[...]


def _moe(h2, cw, x1, g2, per_token, rows_per_seq, wp):
    n, d = x1.shape
    tm = min(MOE_ROW_TILE, n)
    bps = max(rows_per_seq // tm, 1)
    gw = EXPERTS_PER_GROUP * D_EXPERT
    if per_token:
        mod = pl.BlockSpec((tm, d), lambda i, e: (i, 0))
    else:
        mod = pl.BlockSpec((None, 1, d), lambda i, e: (i // bps, 0, 0))
    return pl.pallas_call(
        _moe_kernel,
        out_shape=jax.ShapeDtypeStruct((n, d), F32),
        grid=(n // tm, N_GROUPS),
        in_specs=[pl.BlockSpec((tm, d), lambda i, e: (i, 0)), pl.BlockSpec((tm, LANES), lambda i, e: (i, 0)),
                  pl.BlockSpec((tm, d), lambda i, e: (i, 0)), mod,
                  pl.BlockSpec((None, d, gw), lambda i, e: (e, 0, 0)),
                  pl.BlockSpec((None, d, gw), lambda i, e: (e, 0, 0)),
                  pl.BlockSpec((None, gw, d), lambda i, e: (e, 0, 0))],
        out_specs=pl.BlockSpec((tm, d), lambda i, e: (i, 0)),
        scratch_shapes=[pltpu.VMEM((tm, d), F32)],
        compiler_params=_cparams(("parallel", "arbitrary")),
        name="moe",
    )(h2, cw, x1, g2, wp["w_gate"], wp["w_up"], wp["w_down"])


def _page_specs(n_pages, ppc, rows, width):
    def spec(p):
        return pl.BlockSpec((None, rows, width), lambda s, c, pt: (pt[s * n_pages + c * ppc + p], 0, 0))
    return [spec(p) for p in range(ppc)]


def _pad_rows(x, rows):
    return jnp.concatenate([x, jnp.zeros((rows - x.shape[0], x.shape[1]), x.dtype)], axis=0)


def _online_update(m_ref, l_ref, acc_ref, s, valid, pv_fn):
    if valid is not None:
        s = jnp.where(valid, s, NEG_INF)
    m_old = m_ref[...]
    m_new = jnp.maximum(m_old, jnp.max(s, axis=-1, keepdims=True))
    alpha = jnp.exp(m_old - m_new)
    p = jnp.exp(s - m_new)
    if valid is not None:
        p = jnp.where(valid, p, 0.0)
    m_ref[...] = m_new
    l_ref[...] = alpha * l_ref[...] + jnp.sum(p, axis=-1, keepdims=True)
    acc_ref[...] = alpha * acc_ref[...] + pv_fn(p.astype(BF16))


def _diff_sample_kernel(pt_ref, lam_ref, q_ref, kn_ref, vn_ref, hn_ref, *rest, ppc, page, past_len, n_new, out_scale):
    k_refs, v_refs = rest[:ppc], rest[ppc:2 * ppc]
    o_ref, m_ref, l_ref, acc_ref = rest[2 * ppc:]
    c = pl.program_id(1)
    rows = HA * 2 * 8
    hrows = 2 * 8
    lam = lam_ref[0]

    @pl.when(c == 0)
    def _():
        m_ref[...] = jnp.full(m_ref.shape, NEG_INF, F32)
        l_ref[...] = jnp.zeros_like(l_ref)
        acc_ref[...] = jnp.zeros_like(acc_ref)

    ridx = lax.broadcasted_iota(jnp.int32, (rows, 1), 0)
    qpos = past_len + ridx % 8
    slope = jnp.zeros((rows, 1), F32)
    for h in range(HA):
        slope = jnp.where(ridx // hrows == h, SLOPES_A[h], slope)
    q = q_ref[...]

    def per_head(p, v_heads):
        return jnp.concatenate([_dot(p[h * hrows:(h + 1) * hrows], v_heads[h]) for h in range(HA)], axis=0)

    chunk = ppc * page
    kt = jnp.concatenate([r[...] for r in k_refs], axis=1).astype(BF16)
    v_heads = [jnp.concatenate([r[pl.ds(h, page, stride=HA), :] for r in v_refs], axis=0).astype(BF16)
               for h in range(HA)]
    kpos = c * chunk + lax.broadcasted_iota(jnp.int32, (1, chunk), 1)
    s = _dot(q, kt) - slope * (qpos - kpos).astype(F32)
    _online_update(m_ref, l_ref, acc_ref, s, None, lambda p: per_head(p, v_heads))

    @pl.when(c == pl.num_programs(1) - 1)
    def _():
        kn = _pad_rows(kn_ref[...], LANES).astype(BF16)
        vn = _pad_rows(vn_ref[...], LANES).astype(BF16)
        vn_heads = [vn[:, h * DVA:(h + 1) * DVA] for h in range(HA)]
        npos = past_len + lax.broadcasted_iota(jnp.int32, (1, LANES), 1)
        s_new = _dot_nt(q, kn) - slope * (qpos - npos).astype(F32)
        _online_update(m_ref, l_ref, acc_ref, s_new, (npos <= qpos) & (npos < past_len + n_new),
                       lambda p: per_head(p, vn_heads))
        o_all = acc_ref[...] / l_ref[...]
        for h in range(HA):
            r0 = h * hrows
            o = o_all[r0:r0 + 8] - lam * o_all[r0 + 8:r0 + 16]
            o = o * lax.rsqrt(jnp.mean(o * o, axis=-1, keepdims=True) + RMS_EPS) * hn_ref[...]
            o_ref[:, h * DVA:(h + 1) * DVA] = (o * out_scale).astype(BF16)


def _diff_sample(pt, lam, qbd, k_new, v_new, hn, pool_kt, pool_v, n_pages, past_len, n_new, out_scale):
    s = qbd.shape[0]
    page = pool_kt.shape[2]
    ppc = math.gcd(DIFF_PAGES_PER_STEP, n_pages)
    rows = HA * 2 * 8
    per_seq = lambda shape: pl.BlockSpec((None,) + shape, lambda si, c, pt: (si, 0, 0))
    return pl.pallas_call(
        functools.partial(_diff_sample_kernel, ppc=ppc, page=page, past_len=past_len, n_new=n_new, out_scale=out_scale),
        out_shape=jax.ShapeDtypeStruct((s, 8, HA * DVA), BF16),
        grid_spec=pltpu.PrefetchScalarGridSpec(
            num_scalar_prefetch=1, grid=(s, n_pages // ppc),
            in_specs=[pl.BlockSpec(memory_space=pltpu.SMEM), per_seq((rows, HA * 2 * DA)),
                      per_seq((8, HA * 2 * DA)), per_seq((8, HA * DVA)), _const_spec((1, DVA))]
                     + _page_specs(n_pages, ppc, HA * 2 * DA, page) + _page_specs(n_pages, ppc, page * HA, DVA),
            out_specs=per_seq((8, HA * DVA)),
            scratch_shapes=[pltpu.VMEM((rows, 1), F32), pltpu.VMEM((rows, 1), F32), pltpu.VMEM((rows, DVA), F32)]),
        compiler_params=_cparams(("parallel", "arbitrary")),
        name="diff_sample",
    )(pt, lam, qbd, k_new, v_new, hn, *([pool_kt] * ppc), *([pool_v] * ppc))


def _cmp_sample_kernel(pt_ref, q_ref, kn_ref, vn_ref, w1k_ref, pek_ref, b1k_ref, w2k_ref, b2k_ref,
                       w1v_ref, pev_ref, b1v_ref, w2v_ref, b2v_ref, kng_ref, ones_ref, gsum_ref, *rest,
                       ppc, page, past_len, m_rows, mc_rows, nsb):
    k_refs, v_refs = rest[:ppc], rest[ppc:2 * ppc]
    o_ref, sel_ref, bufk, bufv = rest[2 * ppc:]
    c = pl.program_id(1)

    @pl.when(c == 0)
    def _():
        tail = bufk.shape[0] - past_len
        for buf, new in ((bufk, kn_ref), (bufv, vn_ref)):
            buf[pl.ds(past_len, tail), :] = jnp.zeros((tail, LANES), F32)
            buf[pl.ds(past_len, 8), :] = new[...]

    base = c * (ppc * page)
    for p in range(ppc):
        start = pl.multiple_of(base + p * page, page)
        bufk[pl.ds(start, page), :] = k_refs[p][...].T
        bufv[pl.ds(start, page), :] = v_refs[p][...].T

    @pl.when(c == pl.num_programs(1) - 1)
    def _():
        pad = jnp.zeros((m_rows - mc_rows, LANES), F32)
        k = _compress(bufk, mc_rows, w1k_ref, pek_ref, b1k_ref, w2k_ref, b2k_ref)
        k = _group_norm(k, kng_ref[...], ones_ref[...])
        kcmp_b = jnp.concatenate([k, pad], axis=0).astype(BF16)
        v = _compress(bufv, mc_rows, w1v_ref, pev_ref, b1v_ref, w2v_ref, b2v_ref)
        vcmp_b = jnp.concatenate([v, pad], axis=0).astype(BF16)
        rows = HB * 8
        ridx = lax.broadcasted_iota(jnp.int32, (rows, 1), 0)
        qpos = past_len + ridx % 8
        slope = jnp.zeros((rows, 1), F32)
        for h in range(HB):
            slope = jnp.where(ridx // 8 == h, SLOPES_B[h], slope)
        cend = lax.broadcasted_iota(jnp.int32, (1, m_rows), 1) * CMP_STRIDE + (L_CMP - 1)
        visible = cend <= qpos
        s = _dot_nt(q_ref[...], kcmp_b) - slope * (qpos - cend).astype(F32)
        s = jnp.where(visible, s, NEG_INF)
        e = jnp.where(visible, jnp.exp(s - jnp.max(s, axis=-1, keepdims=True)), 0.0)
        p = e / jnp.maximum(jnp.sum(e, axis=-1, keepdims=True), 1e-30)
        o_ref[...] = _dot(p.astype(BF16), vcmp_b)
        gsum = gsum_ref[...]
        for kvh in range(KVH):
            psum = p[kvh * GB * 8:kvh * GB * 8 + 8]
            for g in range(1, GB):
                psum = psum + p[(kvh * GB + g) * 8:(kvh * GB + g + 1) * 8]
            sel_ref[kvh] = _select_blocks(psum, qpos[:8], gsum, nsb)


def _cmp_sample(pt, q_heads, k_new, v_new, pool_k, pool_v, n_pages, past_len, n_new, wp):
    s = q_heads.shape[0]
    page = pool_k.shape[2]
    ppc = math.gcd(NSA_PAGES_PER_STEP, n_pages)
    tk_pad = _round_up(past_len + n_new, SEL_BLOCK)
    nc = tk_pad // CMP_STRIDE
    nsb = tk_pad // SEL_BLOCK
    m_rows = _round_up(nc, LANES)
    nsbp = _round_up(nsb, LANES)
    mc_rows = _round_up(nc, 8)
    buf_rows = CMP_STRIDE * (mc_rows - 1) + L_CMP
    gsum = ((jnp.arange(m_rows)[:, None] // (SEL_BLOCK // CMP_STRIDE) == jnp.arange(nsbp)[None, :])
            & (jnp.arange(m_rows)[:, None] < nc)).astype(BF16)
    cw = [_const_spec((L_CMP * LANES, KVH * CMP_HID)), _const_spec((1, L_CMP * LANES)), _const_spec((1, KVH * CMP_HID)),
          _const_spec((KVH * CMP_HID, LANES)), _const_spec((1, LANES))]
    per_seq = lambda shape: pl.BlockSpec((None,) + shape, lambda si, c, pt: (si,) + (0,) * len(shape))
    return pl.pallas_call(
        functools.partial(_cmp_sample_kernel, ppc=ppc, page=page, past_len=past_len, m_rows=m_rows, mc_rows=mc_rows,
                          nsb=nsb),
        out_shape=(jax.ShapeDtypeStruct((s, HB * 8, LANES), F32), jax.ShapeDtypeStruct((s, KVH, 8, nsbp), F32)),
        grid_spec=pltpu.PrefetchScalarGridSpec(
            num_scalar_prefetch=1, grid=(s, n_pages // ppc),
            in_specs=[per_seq((HB * 8, LANES)), per_seq((8, LANES)), per_seq((8, LANES))] + cw + cw
                     + [_const_spec((1, LANES)), _const_spec((256, 256)), _const_spec((m_rows, nsbp))]
                     + _page_specs(n_pages, ppc, page, LANES) + _page_specs(n_pages, ppc, page, LANES),
            out_specs=(per_seq((HB * 8, LANES)), per_seq((KVH, 8, nsbp))),
            scratch_shapes=[pltpu.VMEM((buf_rows, LANES), F32), pltpu.VMEM((buf_rows, LANES), F32)]),
        compiler_params=_cparams(("parallel", "arbitrary")),
        name="cmp_sample",
    )(pt, q_heads, k_new, v_new, *wp["cmp_k"], *wp["cmp_v"], wp["g_kc"], wp["ones"], gsum,
      *([pool_k] * ppc), *([pool_v] * ppc))


def _nsa_sample_kernel(pt_ref, q_ref, sel_ref, neg_ref, e_ref, kn_ref, vn_ref, kwin_ref, vwin_ref, kwn_ref, vwn_ref,
                       ocmp_ref, gate_ref, *rest, ppc, page, past_len, n_new, nsb, w_buf):
    k_refs, v_refs = rest[:ppc], rest[ppc:2 * ppc]
    o_ref, m_ref, l_ref, acc_ref, mw_ref, lw_ref, accw_ref = rest[2 * ppc:]
    c = pl.program_id(1)
    rows = HB * 8

    @pl.when(c == 0)
    def _():
        m_ref[...] = jnp.full(m_ref.shape, NEG_INF, F32)
        l_ref[...] = jnp.zeros_like(l_ref)
        acc_ref[...] = jnp.zeros_like(acc_ref)

    ridx = lax.broadcasted_iota(jnp.int32, (rows, 1), 0)
    qpos = past_len + ridx % 8
    slope = jnp.zeros((rows, 1), F32)
    for h in range(HB):
        slope = jnp.where(ridx // 8 == h, SLOPES_B[h], slope)
    q = q_ref[...]
    selm = jnp.concatenate([sel_ref[kvh] for kvh in range(KVH) for _ in range(GB)], axis=0)

    def bias(kpos):
        return slope * (qpos - kpos).astype(F32)

    chunk = ppc * page
    kt = jnp.concatenate([r[...] for r in k_refs], axis=1).astype(BF16)
    vt = jnp.concatenate([r[...] for r in v_refs], axis=1).astype(BF16)
    kpos = c * chunk + lax.broadcasted_iota(jnp.int32, (1, chunk), 1)
    q_aug = jnp.concatenate([q, neg_ref[...]], axis=1)
    kt_aug = jnp.concatenate([kt, e_ref[...]], axis=0)
    _online_update(m_ref, l_ref, acc_ref, _dot(q_aug, kt_aug) - bias(kpos), None, lambda p: _dot_nt(p, vt))

    @pl.when(c == pl.num_programs(1) - 1)
    def _():
        npos = past_len + lax.broadcasted_iota(jnp.int32, (1, LANES), 1)
        is_new = (npos <= qpos) & (npos < past_len + n_new)
        kn = _pad_rows(kn_ref[...], LANES).astype(BF16)
        vn = _pad_rows(vn_ref[...], LANES).astype(BF16)
        blk_new = past_len // SEL_BLOCK
        lane_s = lax.broadcasted_iota(jnp.int32, selm.shape, 1)
        picked = jnp.sum(jnp.where(lane_s == blk_new, selm, 0.0), axis=-1, keepdims=True) > 0.5
        _online_update(m_ref, l_ref, acc_ref, _dot_nt(q, kn) - bias(npos), picked & is_new, lambda p: _dot(p, vn))
        o_sel = acc_ref[...] / jnp.maximum(l_ref[...], 1e-30)
        mw_ref[...] = jnp.full(mw_ref.shape, NEG_INF, F32)
        lw_ref[...] = jnp.zeros_like(lw_ref)
        accw_ref[...] = jnp.zeros_like(accw_ref)
        kwt = kwin_ref[...].astype(BF16)
        vwt = vwin_ref[...].astype(BF16)
        wpos = past_len - w_buf + lax.broadcasted_iota(jnp.int32, (1, w_buf), 1)
        dist = qpos - wpos
        valid = (dist >= 0) & (dist < WINDOW) & (wpos >= 0)
        _online_update(mw_ref, lw_ref, accw_ref, _dot(q, kwt) - bias(wpos), valid, lambda p: _dot_nt(p, vwt))
        kwn = _pad_rows(kwn_ref[...], LANES).astype(BF16)
        vwn = _pad_rows(vwn_ref[...], LANES).astype(BF16)
        _online_update(mw_ref, lw_ref, accw_ref, _dot_nt(q, kwn) - bias(npos), is_new & (qpos - npos < WINDOW),
                       lambda p: _dot(p, vwn))
        o_win = accw_ref[...] / jnp.maximum(lw_ref[...], 1e-30)
        g = gate_ref[...]
        o_ref[...] = g[:, 0:1] * ocmp_ref[...] + g[:, 1:2] * o_sel + g[:, 2:3] * o_win


def _nsa_sample(pt, qbd, sel, k_new, v_new, kwin_t, vwin_t, kw_new, vw_new, ocmp, gates, pool_kt, pool_vt,
                n_pages, past_len, n_new, layer):
    s = qbd.shape[0]
    page = pool_kt.shape[2]
    w_buf = kwin_t.shape[2]
    ppc = math.gcd(NSA_PAGES_PER_STEP, n_pages)
    nsbp = sel.shape[-1]
    nsb = _round_up(past_len + n_new, SEL_BLOCK) // SEL_BLOCK
    rows = HB * 8
    chunk = ppc * page
    nchunk = n_pages // ppc
    bpc = chunk // SEL_BLOCK
    assert bpc <= LANES
    emat = (jnp.arange(chunk)[None, :] // SEL_BLOCK == jnp.arange(LANES)[:, None]).astype(BF16)
    neg = ((sel[..., :past_len // SEL_BLOCK] - 1.0) * MASK_NEG).reshape(s, KVH, 1, 8, nchunk, bpc)
    neg = jnp.broadcast_to(neg, (s, KVH, GB, 8, nchunk, bpc)).transpose(0, 4, 1, 2, 3, 5).reshape(s, nchunk, rows, bpc)
    neg = jnp.pad(neg, ((0, 0), (0, 0), (0, 0), (0, LANES - bpc))).astype(BF16)
    per_seq = lambda shape: pl.BlockSpec((None,) + shape, lambda si, c, pt: (si,) + (0,) * len(shape))
    state = pl.BlockSpec((None, LANES, w_buf), lambda si, c, pt: (layer * s + si, 0, 0))
    scratch = [pltpu.VMEM((rows, 1), F32), pltpu.VMEM((rows, 1), F32), pltpu.VMEM((rows, LANES), F32)]
    return pl.pallas_call(
        functools.partial(_nsa_sample_kernel, ppc=ppc, page=page, past_len=past_len, n_new=n_new, nsb=nsb, w_buf=w_buf),
        out_shape=jax.ShapeDtypeStruct((s, rows, LANES), F32),
        grid_spec=pltpu.PrefetchScalarGridSpec(
            num_scalar_prefetch=1, grid=(s, n_pages // ppc),
            in_specs=[per_seq((rows, LANES)), per_seq((KVH, 8, nsbp)),
                      pl.BlockSpec((None, None, rows, LANES), lambda si, c, pt: (si, c, 0, 0)),
                      _const_spec((LANES, chunk)),
                      per_seq((8, LANES)), per_seq((8, LANES)), state, state, per_seq((8, LANES)), per_seq((8, LANES)),
                      per_seq((rows, LANES)), per_seq((rows, LANES))]
                     + _page_specs(n_pages, ppc, LANES, page) + _page_specs(n_pages, ppc, LANES, page),
            out_specs=per_seq((rows, LANES)),
            scratch_shapes=scratch + scratch),
        compiler_params=_cparams(("parallel", "arbitrary")),
        name="nsa_sample",
    )(pt, qbd, sel, neg, emat, k_new, v_new, kwin_t, vwin_t, kw_new, vw_new, ocmp, gates,
      *([pool_kt] * ppc), *([pool_vt] * ppc))


def _prep_cmp(w1, pe, b1, w2, b2):
    eye = jnp.eye(KVH, dtype=F32)
    w1b = jnp.einsum('jde,hk->jhdke', w1, eye).reshape(L_CMP * KVH * DB, KVH * CMP_HID).astype(BF16)
    pef = jnp.tile(pe[:, None, :], (1, KVH, 1)).reshape(1, L_CMP * KVH * DB)
    w2b = jnp.einsum('ed,hk->hekd', w2, eye).reshape(KVH * CMP_HID, KVH * DB).astype(BF16)
    return (w1b, pef, jnp.tile(b1, KVH).reshape(1, -1), w2b, jnp.tile(b2, KVH).reshape(1, -1))


def _prep_layer(l, w_in, norm1_g, norm2_g, qn_a, kn_a, hn_a, qn_b, kn_cmp, kn_sel, kn_win, cmp_k, cmp_v,
                w_up_a, w_up_b, w_out, w_group, b_group, w_router, b_router, w_e_gate, w_e_up, w_e_down):
    d = w_in.shape[1]
    wi = w_in[l]
    o_qb = 2 * HA * 2 * DA + HA * DVA
    o_kv = o_qb + HB * DB
    o_gb = o_kv + 6 * KVH * DB
    o_gm = o_gb + 3 * HB
    wqb = wi[:, o_qb:o_kv].reshape(d, KVH, GB, DB)
    slot = jnp.zeros((d, KVH, GB, KVH, DB), F32)
    for kvh in range(KVH):
        slot = slot.at[:, kvh, :, kvh, :].set(wqb[:, kvh])
    wqb_ext = slot.reshape(d, HB * LANES)
    wgb = jnp.pad(wi[:, o_gb:o_gm], ((0, 0), (0, LANES - 3 * HB)))
    w_all = jnp.concatenate([wi[:, :o_qb], wqb_ext, wi[:, o_kv:o_gb], wgb, wi[:, o_gm:]], axis=1).astype(BF16)
    assert w_all.shape[1] == _W_COLS
    gq = jnp.zeros((KVH, GB, KVH, DB), F32)
    for kvh in range(KVH):
        gq = gq.at[kvh, :, kvh, :].set(jnp.broadcast_to(qn_b[l], (GB, DB)))
    ones = (jnp.arange(256)[:, None] // 64 == jnp.arange(256)[None, :] // 64).astype(BF16)
    w_r = jnp.concatenate([w_group[l], w_router[l].reshape(d, N_EXPERTS)], axis=1)
    w_r = jnp.pad(w_r, ((0, 0), (0, LANES - w_r.shape[1])))
    w_r_hi = w_r.astype(BF16)

    def by_group(w):
        w = w.reshape(N_GROUPS, EXPERTS_PER_GROUP, d, D_EXPERT).transpose(0, 2, 1, 3)
        return w.reshape(N_GROUPS, d, EXPERTS_PER_GROUP * D_EXPERT).astype(BF16)
    b_r = jnp.pad(jnp.concatenate([b_group[l], b_router[l].reshape(-1)]), (0, LANES - N_GROUPS - N_EXPERTS))
    return {
        "norm1_g": norm1_g[l].reshape(1, d), "norm2_g": norm2_g[l].reshape(1, d), "w_in": w_all,
        "g_qa": jnp.tile(qn_a[l], HA * 2).reshape(1, -1), "g_ka": jnp.tile(kn_a[l], HA * 2).reshape(1, -1),
        "g_qb": gq.reshape(1, -1), "g_ks": jnp.tile(kn_sel[l], KVH).reshape(1, -1),
        "g_kw": jnp.tile(kn_win[l], KVH).reshape(1, -1), "g_kc": jnp.tile(kn_cmp[l], KVH).reshape(1, -1),
        "hn_a": hn_a[l].reshape(1, DVA), "ones": ones,
        "cmp_k": _prep_cmp(*[a[l] for a in cmp_k]), "cmp_v": _prep_cmp(*[a[l] for a in cmp_v]),
        "w_up_a": w_up_a[l].astype(BF16), "w_up_b": w_up_b[l].astype(BF16), "w_out": w_out[l].astype(BF16),
        "w_r_hi": w_r_hi, "w_r_lo": (w_r - w_r_hi.astype(F32)).astype(BF16), "b_r": b_r.reshape(1, LANES),
        "w_gate": by_group(w_e_gate[l]), "w_up": by_group(w_e_up[l]),
        "w_down": w_e_down[l].reshape(N_GROUPS, EXPERTS_PER_GROUP * D_EXPERT, d).astype(BF16),
    }


def kernel(x_prompt, x_sample, cache_diff_k, cache_diff_v, cache_cmp_k, cache_cmp_v, cache_sel_k, cache_sel_v, state_win_k, state_win_v, page_table, c_prompt, c_sample, w_ada, b_ada, norm1_g, norm2_g, w_in, qn_a, kn_a, hn_a, lam_q1, lam_k1, lam_q2, lam_k2, qn_b, kn_cmp, kn_sel, kn_win, cmp_pe_k, cmp_w1_k, cmp_b1_k, cmp_w2_k, cmp_b2_k, cmp_pe_v, cmp_w1_v, cmp_b1_v, cmp_w2_v, cmp_b2_v, w_up_a, w_up_b, w_out, w_group, b_group, w_router, b_router, w_e_gate, w_e_up, w_e_down):
    b, t, d = x_prompt.shape
    s, n_new, _ = x_sample.shape
    depth, n_phys, page = cache_diff_k.shape[:3]
    n_pages = page_table.shape[1]
    past_len = n_pages * page
    w_buf = state_win_k.shape[2]
    assert n_new <= 8 and past_len % SEL_BLOCK == 0 and t % LANES == 0

    n_flat = depth * n_phys

    def page_t(p):
        nd = p.ndim
        return p.transpose(0, 1, *range(3, nd), 2).reshape(n_flat, -1, page)

    pool_diff_kt = page_t(cache_diff_k)
    pool_diff_v = cache_diff_v.reshape(n_flat, page * HA, DVA)
    pool_cmp_k, pool_cmp_v = page_t(cache_cmp_k), page_t(cache_cmp_v)
    pool_sel_kt, pool_sel_vt = page_t(cache_sel_k), page_t(cache_sel_v)
    win_kt = state_win_k.transpose(0, 1, 3, 4, 2).reshape(depth * s, KVH * DB, w_buf)
    win_vt = state_win_v.transpose(0, 1, 3, 4, 2).reshape(depth * s, KVH * DB, w_buf)
    slopes_a = jnp.asarray(SLOPES_A, F32)

    xp = x_prompt.reshape(b * t, d)
    xs = x_sample.reshape(s * n_new, d)
    c_all = jnp.concatenate([c_prompt, c_sample], axis=0)
    new_p = [[] for _ in range(8)]
    new_s = [[] for _ in range(8)]
    for l in range(depth):
        lambda_init = 0.8 - 0.6 * math.exp(-0.3 * l)
        out_scale = 1.0 - lambda_init
        wp = _prep_layer(l, w_in, norm1_g, norm2_g, qn_a, kn_a, hn_a, qn_b, kn_cmp, kn_sel, kn_win,
                         (cmp_w1_k, cmp_pe_k, cmp_b1_k, cmp_w2_k, cmp_b2_k),
                         (cmp_w1_v, cmp_pe_v, cmp_b1_v, cmp_w2_v, cmp_b2_v),
                         w_up_a, w_up_b, w_out, w_group, b_group, w_router, b_router, w_e_gate, w_e_up, w_e_down)
        mod, lam_row = _ada(c_all, w_ada[l], b_ada[l], lam_q1[l], lam_k1[l], lam_q2[l], lam_k2[l], lambda_init)
        lam = lam_row[0, :1]
        mods_p = [m.reshape(b, 1, d) for m in jnp.split(mod[:b], 6, axis=-1)]
        mods_s = [jnp.repeat(m, n_new, axis=0) for m in jnp.split(mod[b:], 6, axis=-1)]

        (qa, ka, va, qb, kc, vc, ks, vs, kw, vw, gbs, gm) = _proj(xp, mods_p[1], mods_p[0], False, t, wp)
        r3 = lambda a: a.reshape(b, t, -1)
        oa = _diff_prompt(r3(qa), r3(ka), r3(va), slopes_a, lam, wp["hn_a"], out_scale)
        ocmp, sel = _cmp_prompt(r3(qb), r3(kc), r3(vc), wp)
        ob = _nsa_prompt(r3(qb), r3(ks), r3(vs), r3(kw), r3(vw), sel, ocmp, r3(gbs))
        x1, h2, cw = _merge(oa.reshape(b * t, -1), ob.reshape(b * t, -1), gm, xp, mods_p[2], mods_p[4], mods_p[3],
                            False, t, wp)
        xp = _moe(h2, cw, x1, mods_p[5], False, t, wp)
        wp_rows = min(WINDOW, t)
        rows_p = [ka.reshape(b, t, HA, 2, DA), va.reshape(b, t, HA, DVA)] + \
                 [a.reshape(b, t, KVH, DB) for a in (kc, vc, ks, vs)] + \
                 [a.reshape(b, t, KVH, DB)[:, -wp_rows:] for a in (kw, vw)]

        (qa, ka, va, qb, kc, vc, ks, vs, kw, vw, gbs, gm) = _proj(xs, mods_s[1], mods_s[0], True, n_new, wp)
        pt = (page_table + l * n_phys).reshape(-1).astype(jnp.int32)
        pad8 = lambda a: jnp.pad(a.reshape(s, n_new, -1), ((0, 0), (0, 8 - n_new), (0, 0)))
        qa8 = pad8(qa)
        own = (jnp.arange(HA * 2 * DA)[None, :] // DA == jnp.arange(HA * 2)[:, None]).astype(BF16)
        qbd = (qa8[:, None, :, :] * own[None, :, None, :]).reshape(s, HA * 2 * 8, HA * 2 * DA)
        oa_s = _diff_sample(pt, lam, qbd, pad8(ka), pad8(va), wp["hn_a"], pool_diff_kt, pool_diff_v,
                            n_pages, past_len, n_new, out_scale)
        oa_s = oa_s[:, :n_new].reshape(s * n_new, HA * DVA)
        q_heads = pad8(qb).reshape(s, 8, HB, LANES).transpose(0, 2, 1, 3)
        ocmp_s, sel_s = _cmp_sample(pt, q_heads.reshape(s, HB * 8, LANES), pad8(kc), pad8(vc), pool_cmp_k, pool_cmp_v, n_pages, past_len, n_new, wp)
        kwin = jnp.concatenate([state_win_k[l].reshape(s, w_buf, -1), kw.reshape(s, n_new, -1)], axis=1)
        vwin = jnp.concatenate([state_win_v[l].reshape(s, w_buf, -1), vw.reshape(s, n_new, -1)], axis=1)
        gates = pad8(gbs)[:, :, :3 * HB].reshape(s, 8, HB, 3).transpose(0, 2, 1, 3).reshape(s, HB * 8, 3)
        gates = jnp.pad(gates, ((0, 0), (0, 0), (0, LANES - 3)))
        ob_s = _nsa_sample(pt, q_heads.reshape(s, HB * 8, LANES), sel_s, pad8(ks), pad8(vs), win_kt, win_vt,
                           pad8(kw), pad8(vw), ocmp_s.reshape(s, HB * 8, LANES), gates, pool_sel_kt, pool_sel_vt,
                           n_pages, past_len, n_new, l)
        ob_s = ob_s.reshape(s, KVH, GB, 8, KVH, DB)
        ob_s = jnp.stack([ob_s[:, kvh, :, :, kvh, :] for kvh in range(KVH)], axis=1)
        ob_s = ob_s.transpose(0, 3, 1, 2, 4)[:, :n_new].reshape(s * n_new, HB * DB).astype(BF16)
        x1, h2, cw = _merge(oa_s, ob_s, gm, xs, mods_s[2], mods_s[4], mods_s[3], True, n_new, wp)
        xs = _moe(h2, cw, x1, mods_s[5], True, n_new, wp)
        rows_s = [ka.reshape(s, n_new, HA, 2, DA), va.reshape(s, n_new, HA, DVA)] + \
                 [a.reshape(s, n_new, KVH, DB) for a in (kc, vc, ks, vs)] + \
                 [kwin[:, -w_buf:].reshape(s, w_buf, KVH, DB), vwin[:, -w_buf:].reshape(s, w_buf, KVH, DB)]
        for i in range(8):
            new_p[i].append(rows_p[i])
            new_s[i].append(rows_s[i])

    outs_p = [jnp.stack(a) for a in new_p]
    outs_s = [jnp.stack(a) for a in new_s]
    return (xp.reshape(b, t, d), xs.reshape(s, n_new, d), *outs_p, *outs_s)
```
